```python
import jax, jax.numpy as jnp
from jax import lax
import numpy as np

D_MODEL = 1024
BATCH = 8
SEQ = 4096
DEPTH = 2
DEC_BATCH = 32
DEC_SEQ = 8
PAST_LEN = 16384
PAGE_SIZE = 128

N_MIXERS = 2
N_GLA_LAYERS = (DEPTH + 1) // 2
N_SB_LAYERS = DEPTH // 2
GLA_HEADS = 4
GLA_KEY_DIM = D_MODEL // 2
GLA_VAL_DIM = D_MODEL
GLA_DK = GLA_KEY_DIM // GLA_HEADS
GLA_DV = GLA_VAL_DIM // GLA_HEADS
GLA_GATE_RANK = 16
GLA_TAU = 16.0
GLA_CHUNK = 64
GLA_IN_DIM = 2 * GLA_KEY_DIM + 2 * GLA_VAL_DIM + GLA_GATE_RANK
SB_HEADS = 16
SB_HEAD_DIM = D_MODEL // SB_HEADS
SB_BLOCK = 128
SB_BIAS_INIT = -6.0
D_FF = ((8 * D_MODEL // 3 + 127) // 128) * 128
PLE_DIM = 256
NORM_EPS = 1e-6

kernel_name = "gla_stickbreaking_macaron_ple_step"

F32 = jnp.float32


def rmsnorm(x, g):
    xf = x.astype(F32)
    y = xf * lax.rsqrt(jnp.mean(xf * xf, axis=-1, keepdims=True) + NORM_EPS)
    return (y * g.astype(F32)).astype(x.dtype)


def swiglu(h, w_in, w_out):
    gate, up = jnp.split(h @ w_in, 2, axis=-1)
    return (jax.nn.silu(gate) * up) @ w_out


def gla_inputs(h, w_in, w_a2, b_a):
    B, T, _ = h.shape
    q, k, v, r, a = jnp.split(h @ w_in, [GLA_KEY_DIM, 2 * GLA_KEY_DIM, 2 * GLA_KEY_DIM + GLA_VAL_DIM,
                                         2 * GLA_KEY_DIM + 2 * GLA_VAL_DIM], axis=-1)
    g = jax.nn.log_sigmoid((a @ w_a2 + b_a).astype(F32)) / GLA_TAU
    def heads(t, d):
        return t.reshape(B, T, GLA_HEADS, d).transpose(0, 2, 1, 3).astype(F32)
    q = heads(q, GLA_DK) * (GLA_DK ** -0.5)
    return q, heads(k, GLA_DK), heads(v, GLA_DV), heads(g, GLA_DK), r


def gla_chunk(S, q, k, v, g):
    C = q.shape[2]
    b = jnp.cumsum(g, axis=2)
    o_inter = jnp.einsum('bhtk,bhkv->bhtv', q * jnp.exp(b), S)
    causal = jnp.tril(jnp.ones((C, C), dtype=bool))
    diff = b[:, :, :, None, :] - b[:, :, None, :, :]
    decay = jnp.exp(jnp.where(causal[:, :, None], diff, -jnp.inf))
    scores = jnp.einsum('bhtsk,bhsk->bhts', q[:, :, :, None, :] * decay, k)
    o = o_inter + jnp.einsum('bhts,bhsv->bhtv', scores, v)
    b_last = b[:, :, -1:, :]
    S_new = (jnp.exp(b_last[:, :, 0, :])[..., None] * S
             + jnp.einsum('bhsk,bhsv->bhkv', k * jnp.exp(b_last - b), v))
    return S_new, o


def gla_output(o, r, norm_g, w_out):
    B, H, T, dv = o.shape
    o = rmsnorm(o.transpose(0, 2, 1, 3), norm_g.reshape(H, dv))
    o = o.astype(r.dtype).reshape(B, T, H * dv) * jax.nn.silu(r)
    return o @ w_out


def gla_prompt(h, w_in, w_a2, b_a, norm_g, w_out, state_dtype):
    q, k, v, g, r = gla_inputs(h, w_in, w_a2, b_a)
    B, H, T, _ = q.shape
    nc = T // GLA_CHUNK
    def to_chunks(t):
        return t.reshape(B, H, nc, GLA_CHUNK, t.shape[-1]).transpose(2, 0, 1, 3, 4)
    S0 = jnp.zeros((B, H, GLA_DK, GLA_DV), F32)
    S_fin, o = lax.scan(lambda S, c: gla_chunk(S, *c), S0,
                        (to_chunks(q), to_chunks(k), to_chunks(v), to_chunks(g)))
    o = o.transpose(1, 2, 0, 3, 4).reshape(B, H, T, GLA_DV)
    return gla_output(o, r, norm_g, w_out), S_fin.astype(state_dtype)


def gla_sample(h, state, w_in, w_a2, b_a, norm_g, w_out):
    q, k, v, g, r = gla_inputs(h, w_in, w_a2, b_a)
    S_new, o = gla_chunk(state.astype(F32), q, k, v, g)
    return gla_output(o, r, norm_g, w_out), S_new.astype(state.dtype)


def sb_qkv(h, w_qkv):
    B, T, _ = h.shape
    q, k, v = jnp.split(h @ w_qkv, 3, axis=-1)
    shp = (B, T, SB_HEADS, SB_HEAD_DIM)
    return q.reshape(shp), k.reshape(shp), v.reshape(shp)


def sb_prompt(q, k, v, bias):
    B, T, H, Dh = q.shape
    nb = T // SB_BLOCK
    scale = Dh ** -0.5
    kpos = jnp.arange(T)
    vf = v.astype(F32)
    bh = bias.astype(F32)[None, :, None, None]
    q_blocks = q.reshape(B, nb, SB_BLOCK, H, Dh).transpose(1, 0, 2, 3, 4)
    def one_block(args):
        qb, bi = args
        z = jnp.einsum('bqhd,bshd->bhqs', qb, k).astype(F32) * scale + bh
        qpos = bi * SB_BLOCK + jnp.arange(SB_BLOCK)
        causal = kpos[None, :] < qpos[:, None]
        log_1m = jnp.where(causal, jax.nn.log_sigmoid(-z), 0.0)
        after = lax.cumsum(log_1m, axis=3, reverse=True) - log_1m
        A = jnp.where(causal, jnp.exp(jax.nn.log_sigmoid(z) + after), 0.0)
        return jnp.einsum('bhqs,bshd->bqhd', A, vf)
    o = lax.map(one_block, (q_blocks, jnp.arange(nb)))
    return o.transpose(1, 0, 2, 3, 4).reshape(B, T, H, Dh).astype(q.dtype)


def sb_sample(q, k, v, bias, cache_k, cache_v, layer, page_table):
    DB, T, H, Dh = q.shape
    scale = Dh ** -0.5
    bh = bias.astype(F32)[None, :, None, None]
    z = jnp.einsum('bthd,bshd->bhts', q, k).astype(F32) * scale + bh
    causal = jnp.arange(T)[None, :] < jnp.arange(T)[:, None]
    log_1m = jnp.where(causal, jax.nn.log_sigmoid(-z), 0.0)
    after = lax.cumsum(log_1m, axis=3, reverse=True) - log_1m
    A = jnp.where(causal, jnp.exp(jax.nn.log_sigmoid(z) + after), 0.0)
    acc = jnp.einsum('bhts,bshd->bthd', A, v.astype(F32))
    run = jnp.sum(log_1m, axis=-1)
    def page_step(carry, phys):
        acc, run = carry
        kp = cache_k[layer, phys]
        vp = cache_v[layer, phys]
        zp = jnp.einsum('bthd,bphd->bhtp', q, kp).astype(F32) * scale + bh
        lp = jax.nn.log_sigmoid(-zp)
        after_p = lax.cumsum(lp, axis=3, reverse=True) - lp + run[..., None]
        Ap = jnp.exp(jax.nn.log_sigmoid(zp) + after_p)
        acc = acc + jnp.einsum('bhtp,bphd->bthd', Ap, vp.astype(F32))
        return (acc, run + jnp.sum(lp, axis=-1)), None
    (acc, _), _ = lax.scan(page_step, (acc, run), page_table[:, ::-1].T)
    return acc.astype(q.dtype)


def setup_inputs(seed: int = 0) -> dict:
    key = jax.random.key(seed)
    ks = iter(jax.random.split(key, 32))
    n_pages = PAST_LEN // PAGE_SIZE
    n_phys = (5 * DEC_BATCH * n_pages + 3) // 4
    def nrm(shape, s=1.0):
        return jax.random.normal(next(ks), shape, F32) * s
    def w(shape, fan_in):
        return nrm(shape, fan_in ** -0.5)
    def gain(shape):
        return 1.0 + nrm(shape, 0.02)
    return {
        "x_prompt": nrm((BATCH, SEQ, D_MODEL)),
        "x_sample": nrm((DEC_BATCH, DEC_SEQ, D_MODEL)),
        "state_gla": nrm((N_GLA_LAYERS, DEC_BATCH, GLA_HEADS, GLA_DK, GLA_DV), 0.5),
        "cache_k": nrm((N_SB_LAYERS, n_phys, PAGE_SIZE, SB_HEADS, SB_HEAD_DIM)),
        "cache_v": nrm((N_SB_LAYERS, n_phys, PAGE_SIZE, SB_HEADS, SB_HEAD_DIM)),
        "page_table": jax.random.permutation(next(ks), n_phys)[:DEC_BATCH * n_pages]
                          .reshape(DEC_BATCH, n_pages).astype(jnp.int32),
        "p_prompt": nrm((DEPTH, BATCH, SEQ, PLE_DIM)),
        "p_sample": nrm((DEPTH, DEC_BATCH, DEC_SEQ, PLE_DIM)),
        "norm_ffn1": gain((DEPTH, D_MODEL)),
        "ffn1_w_in": w((DEPTH, D_MODEL, 2 * D_FF), D_MODEL),
        "ffn1_w_out": w((DEPTH, D_FF, D_MODEL), D_FF),
        "norm_mix": gain((DEPTH, D_MODEL)),
        "gla_w_in": w((N_GLA_LAYERS, D_MODEL, GLA_IN_DIM), D_MODEL),
        "gla_w_a2": w((N_GLA_LAYERS, GLA_GATE_RANK, GLA_KEY_DIM), GLA_GATE_RANK),
        "gla_b_a": nrm((N_GLA_LAYERS, GLA_KEY_DIM), 0.1),
        "gla_norm": gain((N_GLA_LAYERS, GLA_VAL_DIM)),
        "gla_w_out": w((N_GLA_LAYERS, GLA_VAL_DIM, D_MODEL), GLA_VAL_DIM),
        "sb_w_qkv": w((N_SB_LAYERS, D_MODEL, 3 * D_MODEL), D_MODEL),
        "sb_bias": SB_BIAS_INIT + nrm((N_SB_LAYERS, SB_HEADS), 0.1),
        "sb_w_out": w((N_SB_LAYERS, D_MODEL, D_MODEL), D_MODEL),
        "norm_ffn2": gain((DEPTH, D_MODEL)),
        "ffn2_w_in": w((DEPTH, D_MODEL, 2 * D_FF), D_MODEL),
        "ffn2_w_out": w((DEPTH, D_FF, D_MODEL), D_FF),
        "norm_pe": gain((DEPTH, D_MODEL)),
        "pe_w_gate": w((DEPTH, D_MODEL, D_MODEL), D_MODEL),
        "pe_w_proj": w((DEPTH, PLE_DIM, D_MODEL), PLE_DIM),
        "norm_final": gain((D_MODEL,)),
    }


def reference(x_prompt, x_sample, state_gla, cache_k, cache_v, page_table, p_prompt, p_sample,
              norm_ffn1, ffn1_w_in, ffn1_w_out, norm_mix, gla_w_in, gla_w_a2, gla_b_a, gla_norm,
              gla_w_out, sb_w_qkv, sb_bias, sb_w_out, norm_ffn2, ffn2_w_in, ffn2_w_out, norm_pe,
              pe_w_gate, pe_w_proj, norm_final):
    def ffn_pre(x, i):
        return x + 0.5 * swiglu(rmsnorm(x, norm_ffn1[i]), ffn1_w_in[i], ffn1_w_out[i])

    def ffn_post(x, i):
        return x + 0.5 * swiglu(rmsnorm(x, norm_ffn2[i]), ffn2_w_in[i], ffn2_w_out[i])

    def per_layer_embed(x, p_i, i):
        gate = jax.nn.sigmoid(rmsnorm(x, norm_pe[i]) @ pe_w_gate[i])
        return x + gate * (p_i @ pe_w_proj[i])

    gla_p, gla_s, kp_list, vp_list, ks_list, vs_list = [], [], [], [], [], []
    xp, xs = x_prompt, x_sample
    for i in range(DEPTH):
        j = i // N_MIXERS
        xp = ffn_pre(xp, i)
        xs = ffn_pre(xs, i)
        hp = rmsnorm(xp, norm_mix[i])
        hs = rmsnorm(xs, norm_mix[i])
        if i % N_MIXERS == 0:
            mp, Sp = gla_prompt(hp, gla_w_in[j], gla_w_a2[j], gla_b_a[j], gla_norm[j], gla_w_out[j],
                                state_gla.dtype)
            ms, Ss = gla_sample(hs, state_gla[j], gla_w_in[j], gla_w_a2[j], gla_b_a[j], gla_norm[j],
                                gla_w_out[j])
            gla_p.append(Sp)
            gla_s.append(Ss)
        else:
            qp, kp, vp = sb_qkv(hp, sb_w_qkv[j])
            qs, ks_, vs_ = sb_qkv(hs, sb_w_qkv[j])
            Bp, Tp = hp.shape[:2]
            Bs, Ts = hs.shape[:2]
            mp = sb_prompt(qp, kp, vp, sb_bias[j]).reshape(Bp, Tp, D_MODEL) @ sb_w_out[j]
            ms = sb_sample(qs, ks_, vs_, sb_bias[j], cache_k, cache_v, j,
                           page_table).reshape(Bs, Ts, D_MODEL) @ sb_w_out[j]
            kp_list.append(kp.astype(cache_k.dtype))
            vp_list.append(vp.astype(cache_v.dtype))
            ks_list.append(ks_.astype(cache_k.dtype))
            vs_list.append(vs_.astype(cache_v.dtype))
        xp = ffn_post(xp + mp, i)
        xs = ffn_post(xs + ms, i)
        xp = per_layer_embed(xp, p_prompt[i], i)
        xs = per_layer_embed(xs, p_sample[i], i)
    y_prompt = rmsnorm(xp, norm_final)
    y_sample = rmsnorm(xs, norm_final)
    return (y_prompt, y_sample, jnp.stack(gla_p), jnp.stack(gla_s), jnp.stack(kp_list),
            jnp.stack(vp_list), jnp.stack(ks_list), jnp.stack(vs_list))
```

```python
import functools

import jax
import jax.numpy as jnp
from jax import lax
from jax.experimental import pallas as pl
from jax.experimental.pallas import tpu as pltpu

F32 = jnp.float32
BF16 = jnp.bfloat16

NORM_EPS = 1e-6
GLA_TAU = 16.0
GLA_CHUNK = 64
SB_BLOCK = 256
PAGES_PER_STEP = 4
LANES = 128
VMEM_LIMIT = 56 * 1024 * 1024


def _params(*sem):
    return pltpu.CompilerParams(dimension_semantics=sem, vmem_limit_bytes=VMEM_LIMIT)


def _rms(x, g):
    return x * lax.rsqrt(jnp.mean(x * x, axis=-1, keepdims=True) + NORM_EPS) * g


def _dot(a, b):
    return jnp.dot(a, b, preferred_element_type=F32)


def _dot_nt(a, b):
    return lax.dot_general(a, b, (((1,), (1,)), ((), ())), preferred_element_type=F32)


def _dot_tn(a, b):
    return lax.dot_general(a, b, (((0,), (0,)), ((), ())), preferred_element_type=F32)


def _softplus(z):
    return jnp.maximum(z, 0.0) + jnp.log(1.0 + jnp.exp(-jnp.abs(z)))


def _split_bf16(x):
    hi = x.astype(BF16)
    lo = (x - hi.astype(F32)).astype(BF16)
    return hi, lo


def _const_spec(a):
    nd = a.ndim
    return pl.BlockSpec(a.shape, lambda *_: (0,) * nd)


def _row_call(body, row_args, const_args, out_dims, tm, **kw):
    n = row_args[0].shape[0]
    tm = min(tm, n)
    assert n % tm == 0
    in_specs = [pl.BlockSpec((tm, a.shape[1]), lambda i: (i, 0)) for a in row_args]
    in_specs += [_const_spec(a) for a in const_args]
    out_specs = [pl.BlockSpec((tm, d), lambda i: (i, 0)) for d, _ in out_dims]
    out_shape = [jax.ShapeDtypeStruct((n, d), dt) for d, dt in out_dims]
    outs = pl.pallas_call(
        functools.partial(body, **kw),
        grid=(n // tm,),
        in_specs=in_specs,
        out_specs=out_specs,
        out_shape=out_shape,
        compiler_params=_params("parallel"),
    )(*row_args, *const_args)
    return outs


def _ffn_body(x_ref, g_ref, win_ref, wout_ref, o_ref):
    x = x_ref[...]
    xn = _rms(x, g_ref[...]).astype(BF16)
    h = _dot(xn, win_ref[...])
    dff = wout_ref.shape[0]
    gate, up = h[:, :dff], h[:, dff:]
    act = (gate * jax.nn.sigmoid(gate) * up).astype(BF16)
    o_ref[...] = x + 0.5 * _dot(act, wout_ref[...])


def _ple_body(x_ref, p_ref, g_ref, wg_ref, wp_ref, gf_ref, o_ref, *, final):
    x = x_ref[...]
    xn = _rms(x, g_ref[...]).astype(BF16)
    gate = jax.nn.sigmoid(_dot(xn, wg_ref[...]))
    y = x + gate * _dot(p_ref[...].astype(BF16), wp_ref[...])
    if final:
        y = _rms(y, gf_ref[...])
    o_ref[...] = y


def _gla_proj_body(x_ref, g_ref, w_ref, wa_ref, wa2_ref, ba_ref,
                   q_ref, k_ref, v_ref, r_ref, gate_ref, *, dk_total, dv_total, q_scale):
    xn = _rms(x_ref[...], g_ref[...]).astype(BF16)
    y = _dot(xn, w_ref[...])
    q_ref[...] = y[:, :dk_total] * q_scale
    k_ref[...] = y[:, dk_total:2 * dk_total]
    v_ref[...] = y[:, 2 * dk_total:2 * dk_total + dv_total]
    r_ref[...] = y[:, 2 * dk_total + dv_total:]
    a = _dot(xn, wa_ref[...]).astype(BF16)
    ga = _dot(a, wa2_ref[...]) + ba_ref[...]
    gate_ref[...] = (jnp.minimum(ga, 0.0) - jnp.log1p(jnp.exp(-jnp.abs(ga)))) * (1.0 / GLA_TAU)


def _sb_proj_body(x_ref, g_ref, w_ref, q_ref, k_ref, v_ref, *, d, q_scale):
    xn = _rms(x_ref[...], g_ref[...]).astype(BF16)
    y = _dot(xn, w_ref[...])
    q_ref[...] = (y[:, :d] * q_scale).astype(BF16)
    k_ref[...] = y[:, d:2 * d]
    v_ref[...] = y[:, 2 * d:]


def _sb_proj_t_body(x_ref, g_ref, wq_ref, wkt_ref, wvt_ref, q_ref, kt_ref, vt_ref, ktb_ref, vtb_ref,
                    *, q_scale):
    xn = _rms(x_ref[...], g_ref[...]).astype(BF16)
    q_ref[...] = (_dot(xn, wq_ref[...]) * q_scale).astype(BF16)
    blk = ktb_ref.shape[3]
    for w_ref, t_ref, tb_ref in ((wkt_ref, kt_ref, ktb_ref), (wvt_ref, vt_ref, vtb_ref)):
        yt = _dot_nt(w_ref[...], xn)
        t_ref[0] = yt
        for j in range(tb_ref.shape[1]):
            tb_ref[0, j] = yt[:, j * blk:(j + 1) * blk].astype(BF16)


def _sb_proj_t(x, g, wq, wkt, wvt, *, batch, tm, blk, q_scale):
    n, d = x.shape
    seq = n // batch
    tm = min(tm, seq)
    nt = seq // tm
    row = pl.BlockSpec((tm, d), lambda b, t: (b * nt + t, 0))
    tr = pl.BlockSpec((1, d, tm), lambda b, t: (b, 0, t))
    trb = pl.BlockSpec((1, tm // blk, d, blk), lambda b, t: (b, t, 0, 0))
    return pl.pallas_call(
        functools.partial(_sb_proj_t_body, q_scale=q_scale),
        grid=(batch, nt),
        in_specs=[row, _const_spec(g), _const_spec(wq), _const_spec(wkt), _const_spec(wvt)],
        out_specs=[row, tr, tr, trb, trb],
        out_shape=[jax.ShapeDtypeStruct((n, d), BF16),
                   jax.ShapeDtypeStruct((batch, d, seq), F32),
                   jax.ShapeDtypeStruct((batch, d, seq), F32),
                   jax.ShapeDtypeStruct((batch, seq // blk, d, blk), BF16),
                   jax.ShapeDtypeStruct((batch, seq // blk, d, blk), BF16)],
        compiler_params=_params("parallel", "parallel"),
    )(x, g, wq, wkt, wvt)


def _out_proj_body(x_ref, o_ref, w_ref, y_ref):
    y_ref[...] = x_ref[...] + _dot(o_ref[...].astype(BF16), w_ref[...])


def _gla_mix_body(x_ref, q_ref, k_ref, v_ref, r_ref, g_ref, s0_ref, gn_ref, wo_ref, tri_ref,
                  y_ref, sfin_ref, st_ref, o_ref, *, heads, chunk):
    t = pl.program_id(1)
    tile = x_ref.shape[0]
    dk = q_ref.shape[1] // heads
    dv = v_ref.shape[1] // heads

    @pl.when(t == 0)
    def _():
        for h in range(heads):
            st_ref[h] = s0_ref[0, h].T

    tri = tri_ref[...]
    causal = (lax.broadcasted_iota(jnp.int32, (chunk, chunk), 1)
              <= lax.broadcasted_iota(jnp.int32, (chunk, chunk), 0))

    def chunk_step(c, carry):
        rows = pl.ds(pl.multiple_of(c * chunk, chunk), chunk)
        g_hi, g_lo = _split_bf16(g_ref[rows, :])
        b = _dot(tri, g_hi) + _dot(tri, g_lo)
        q = q_ref[rows, :]
        k = k_ref[rows, :]
        v = v_ref[rows, :].astype(BF16)
        for h in range(heads):
            ks = slice(h * dk, (h + 1) * dk)
            vs = slice(h * dv, (h + 1) * dv)
            bh = b[:, ks]
            b_last = bh[chunk - 1:chunk, :]
            q_t = (q[:, ks] * jnp.exp(bh)).astype(BF16)
            k_t = (k[:, ks] * jnp.exp(-bh)).astype(BF16)
            k_s = (k[:, ks] * jnp.exp(b_last - bh)).astype(BF16)
            s_t = st_ref[h]
            o = _dot_nt(q_t, s_t.astype(BF16))
            sc = jnp.where(causal, _dot_nt(q_t, k_t), 0.0)
            o = o + _dot(sc.astype(BF16), v[:, vs])
            o_ref[rows, vs] = o
            st_ref[h] = s_t * jnp.exp(b_last) + _dot_tn(v[:, vs], k_s)
        return carry

    lax.fori_loop(0, tile // chunk, chunk_step, 0)

    gn = gn_ref[...]
    r = r_ref[...]
    parts = []
    for h in range(heads):
        vs = slice(h * dv, (h + 1) * dv)
        parts.append(_rms(o_ref[:, vs], gn[:, vs]))
    on = jnp.concatenate(parts, axis=1)
    on = (on * (r * jax.nn.sigmoid(r))).astype(BF16)
    y_ref[...] = x_ref[...] + _dot(on, wo_ref[...])

    @pl.when(t == pl.num_programs(1) - 1)
    def _():
        for h in range(heads):
            sfin_ref[0, h] = st_ref[h].T


def _gla_mix(x, q, k, v, r, g, s0, gn, wo, *, batch, chunk, tile):
    n, d = x.shape
    seq = n // batch
    heads, dk, dv = s0.shape[1:]
    nt = seq // tile
    tri = jnp.tril(jnp.ones((chunk, chunk), F32)).astype(BF16)
    row = lambda w: pl.BlockSpec((tile, w), lambda b, t: (b * nt + t, 0))
    y, sfin = pl.pallas_call(
        functools.partial(_gla_mix_body, heads=heads, chunk=chunk),
        grid=(batch, nt),
        in_specs=[row(d), row(heads * dk), row(heads * dk), row(heads * dv), row(heads * dv),
                  row(heads * dk),
                  pl.BlockSpec((1, heads, dk, dv), lambda b, t: (b, 0, 0, 0)),
                  _const_spec(gn), _const_spec(wo), _const_spec(tri)],
        out_specs=[row(d), pl.BlockSpec((1, heads, dk, dv), lambda b, t: (b, 0, 0, 0))],
        out_shape=[jax.ShapeDtypeStruct((n, d), F32),
                   jax.ShapeDtypeStruct((batch, heads, dk, dv), F32)],
        scratch_shapes=[pltpu.VMEM((heads, dv, dk), F32), pltpu.VMEM((tile, heads * dv), F32)],
        compiler_params=_params("arbitrary", "arbitrary"),
    )(x, q, k, v, r, g, s0, gn, wo, tri)
    return y, sfin


def _sb_block(z, run, vt, tri, causal):
    sp = _softplus(z)
    if causal is not None:
        sp = jnp.where(causal, sp, 0.0)
    hi, lo = _split_bf16(sp)
    later = _dot(hi, tri) + _dot(lo, tri)
    a = jnp.exp(z - sp - later + run)
    if causal is not None:
        a = jnp.where(causal, a, 0.0)
    total = later[:, 0:1] + sp[:, 0:1]
    return _dot_nt(a.astype(BF16), vt), total


def _sb_prompt_body(bias_ref, q_ref, k_ref, v_ref, tri_ref, o_ref, *, blk, hd):
    h2 = pl.program_id(1)
    qi = pl.program_id(2)
    q2 = q_ref[0]
    tri = tri_ref[...]
    lane = lax.broadcasted_iota(jnp.int32, q2.shape, 1)
    causal = (lax.broadcasted_iota(jnp.int32, (blk, blk), 1)
              < lax.broadcasted_iota(jnp.int32, (blk, blk), 0))
    heads_per_tile = q2.shape[1] // hd
    outs = []
    for i in range(heads_per_tile):
        in_head = (lane >= i * hd) & (lane < (i + 1) * hd)
        qh = jnp.where(in_head, q2, jnp.zeros_like(q2))
        bias = bias_ref[h2 * heads_per_tile + i]

        def block(kb, acc, run, mask):
            z = _dot(qh, k_ref[0, kb]) + bias
            av, total = _sb_block(z, run, v_ref[0, kb], tri, mask)
            return acc + av, run - total

        acc, run = block(qi, jnp.zeros(q2.shape, F32), jnp.zeros((blk, 1), F32), causal)
        acc, run = lax.fori_loop(
            0, qi, lambda j, c: block(qi - 1 - j, c[0], c[1], None), (acc, run))
        outs.append((in_head, acc))
    o = outs[0][1]
    for in_head, acc in outs[1:]:
        o = jnp.where(in_head, acc, o)
    o_ref[0] = o.astype(o_ref.dtype)


def _sb_prompt(q, kt, vt, bias, *, batch, hd):
    n, d = q.shape
    seq = n // batch
    nq, blk = kt.shape[1], kt.shape[3]
    q3 = q.reshape(batch, seq, d)
    tri = (jnp.arange(blk)[:, None] > jnp.arange(blk)[None, :]).astype(BF16)
    kv_spec = pl.BlockSpec((1, nq, LANES, blk), lambda b, h, i: (b, 0, h, 0))
    o = pl.pallas_call(
        functools.partial(_sb_prompt_body, blk=blk, hd=hd),
        grid=(batch, d // LANES, nq),
        in_specs=[pl.BlockSpec(memory_space=pltpu.SMEM),
                  pl.BlockSpec((1, blk, LANES), lambda b, h, i: (b, i, h)),
                  kv_spec, kv_spec, _const_spec(tri)],
        out_specs=pl.BlockSpec((1, blk, LANES), lambda b, h, i: (b, i, h)),
        out_shape=jax.ShapeDtypeStruct((batch, seq, d), BF16),
        compiler_params=_params("parallel", "parallel", "arbitrary"),
    )(bias, q3, kt, vt, tri)
    return o.reshape(n, d)


def _sb_sample_body(pt_ref, qbd_ref, bias_ref, kn_ref, vn_ref, tri_ref, *rest, pages, heads, hd):
    k_refs = rest[:pages]
    v_refs = rest[pages:2 * pages]
    o_ref, acc_ref, run_ref, kpad_ref, vpad_ref = rest[2 * pages:]
    p = pl.program_id(1)
    t_new = kn_ref.shape[1]
    rows = qbd_ref.shape[1]
    psz = tri_ref.shape[0]
    qbd = qbd_ref[0]
    bias = bias_ref[...]
    tri = tri_ref[...]

    def page(kp, vp, causal):
        z = (_dot(qbd, kp) if causal is None else _dot_nt(qbd, kp)) + bias
        sp = _softplus(z)
        if causal is not None:
            sp = jnp.where(causal, sp, 0.0)
        hi, lo = _split_bf16(sp)
        both = _dot(hi, tri) + _dot(lo, tri)
        later, total = both[:, :psz], both[:, psz:]
        a = jnp.exp(z - sp - later + run_ref[...])
        if causal is not None:
            a = jnp.where(causal, a, 0.0)
        a = a.astype(BF16)
        acc_ref[...] += _dot_nt(a, vp) if causal is None else _dot(a, vp)
        run_ref[...] -= total

    @pl.when(p == 0)
    def _():
        acc_ref[...] = jnp.zeros_like(acc_ref)
        run_ref[...] = jnp.zeros_like(run_ref)
        kpad_ref[...] = jnp.zeros_like(kpad_ref)
        vpad_ref[...] = jnp.zeros_like(vpad_ref)
        kpad_ref[0:t_new, :] = kn_ref[0]
        vpad_ref[0:t_new, :] = vn_ref[0]
        tq = lax.broadcasted_iota(jnp.int32, (rows, psz), 0) % t_new
        causal = lax.broadcasted_iota(jnp.int32, (rows, psz), 1) < tq
        page(kpad_ref[...].astype(BF16), vpad_ref[...].astype(BF16), causal)

    for i in range(pages):
        page(k_refs[i][0].astype(BF16), v_refs[i][0].astype(BF16), None)

    @pl.when(p == pl.num_programs(1) - 1)
    def _():
        lane = lax.broadcasted_iota(jnp.int32, (t_new, LANES), 1)
        per_tile = LANES // hd
        for gidx in range(heads // per_tile):
            cols = slice(gidx * LANES, (gidx + 1) * LANES)
            o = acc_ref[gidx * per_tile * t_new:(gidx * per_tile + 1) * t_new, cols]
            for i in range(1, per_tile):
                h = gidx * per_tile + i
                o = jnp.where(lane >= i * hd, acc_ref[h * t_new:(h + 1) * t_new, cols], o)
            o_ref[0, :, cols] = o


def _sb_sample(q, k_new, v_new, bias, cache_k, cache_v, layer, page_table, *, heads, hd):
    db, t_new, d = q.shape
    n_layers, n_phys, psz = cache_k.shape[:3]
    n_pages = page_table.shape[1]
    pages = PAGES_PER_STEP if n_pages % PAGES_PER_STEP == 0 else 1
    ck = cache_k.transpose(0, 1, 3, 4, 2).reshape(n_layers * n_phys, d, psz)
    cv = cache_v.transpose(0, 1, 3, 4, 2).reshape(n_layers * n_phys, d, psz)
    head_of_lane = jnp.arange(d) // hd
    qbd = jnp.where(head_of_lane[None, None, None, :] == jnp.arange(heads)[None, :, None, None],
                    q[:, None, :, :], jnp.zeros((), q.dtype)).reshape(db, heads * t_new, d)
    bias_rows = jnp.broadcast_to(jnp.repeat(bias.astype(F32), t_new)[:, None], (heads * t_new, psz))
    tri = jnp.concatenate([(jnp.arange(psz)[:, None] > jnp.arange(psz)[None, :]).astype(BF16),
                           jnp.ones((psz, psz), BF16)], axis=1)
    base = layer * n_phys

    def page_spec(i):
        return pl.BlockSpec(
            (1, d, psz), lambda b, p, pt: (pt[b, n_pages - 1 - (p * pages + i)] + base, 0, 0))

    per_seq = lambda shape: pl.BlockSpec(shape, lambda b, p, pt: (b, 0, 0))
    const = lambda a: pl.BlockSpec(a.shape, lambda b, p, pt: (0,) * a.ndim)
    grid_spec = pltpu.PrefetchScalarGridSpec(
        num_scalar_prefetch=1,
        grid=(db, n_pages // pages),
        in_specs=[per_seq((1, heads * t_new, d)), const(bias_rows),
                  per_seq((1, t_new, d)), per_seq((1, t_new, d)), const(tri)]
                 + [page_spec(i) for i in range(pages)] * 2,
        out_specs=per_seq((1, t_new, d)),
        scratch_shapes=[pltpu.VMEM((heads * t_new, d), F32), pltpu.VMEM((heads * t_new, psz), F32),
                        pltpu.VMEM((psz, d), F32), pltpu.VMEM((psz, d), F32)],
    )
    return pl.pallas_call(
        functools.partial(_sb_sample_body, pages=pages, heads=heads, hd=hd),
        grid_spec=grid_spec,
        out_shape=jax.ShapeDtypeStruct((db, t_new, d), F32),
        compiler_params=_params("parallel", "arbitrary"),
    )(page_table, qbd, bias_rows, k_new, v_new, tri, *([ck] * pages), *([cv] * pages))


def kernel(x_prompt, x_sample, state_gla, cache_k, cache_v, page_table, p_prompt, p_sample,
           norm_ffn1, ffn1_w_in, ffn1_w_out, norm_mix, gla_w_in, gla_w_a2, gla_b_a, gla_norm,
           gla_w_out, sb_w_qkv, sb_bias, sb_w_out, norm_ffn2, ffn2_w_in, ffn2_w_out, norm_pe,
           pe_w_gate, pe_w_proj, norm_final):
    bp, tp, d = x_prompt.shape
    bs, ts, _ = x_sample.shape
    depth = norm_ffn1.shape[0]
    n_mixers = 2
    gla_heads, gla_dk, gla_dv = state_gla.shape[2:]
    dk_total, dv_total = gla_heads * gla_dk, gla_heads * gla_dv
    rank = gla_w_a2.shape[1]
    sb_heads, sb_hd = cache_k.shape[3:]
    row = lambda a: a.reshape(1, -1)

    groups = [dict(x=x_prompt.reshape(bp * tp, d), p=p_prompt.reshape(depth, bp * tp, -1), batch=bp),
              dict(x=x_sample.reshape(bs * ts, d), p=p_sample.reshape(depth, bs * ts, -1), batch=bs)]
    gla_states = [[], []]
    new_k = [[], []]
    new_v = [[], []]

    for i in range(depth):
        j = i // n_mixers
        w1_in, w1_out = ffn1_w_in[i].astype(BF16), ffn1_w_out[i].astype(BF16)
        w2_in, w2_out = ffn2_w_in[i].astype(BF16), ffn2_w_out[i].astype(BF16)
        wg, wp = pe_w_gate[i].astype(BF16), pe_w_proj[i].astype(BF16)
        if i % n_mixers == 0:
            w_main = gla_w_in[j, :, :2 * dk_total + 2 * dv_total].astype(BF16)
            w_a = jnp.pad(gla_w_in[j, :, 2 * dk_total + 2 * dv_total:],
                          ((0, 0), (0, LANES - rank))).astype(BF16)
            w_a2 = jnp.pad(gla_w_a2[j], ((0, LANES - rank), (0, 0))).astype(BF16)
            w_o = gla_w_out[j].astype(BF16)
        else:
            w_qkv = sb_w_qkv[j].astype(BF16)
            w_o = sb_w_out[j].astype(BF16)

        for gi, grp in enumerate(groups):
            x = grp["x"]
            is_prompt = gi == 0
            (x,) = _row_call(_ffn_body, [x], [row(norm_ffn1[i]), w1_in, w1_out], [(d, F32)], 512)
            if i % n_mixers == 0:
                q, k, v, r, g = _row_call(
                    _gla_proj_body, [x], [row(norm_mix[i]), w_main, w_a, w_a2, row(gla_b_a[j])],
                    [(dk_total, F32), (dk_total, F32), (dv_total, F32), (dv_total, F32), (dk_total, F32)],
                    512, dk_total=dk_total, dv_total=dv_total, q_scale=gla_dk ** -0.5)
                if is_prompt:
                    s0 = jnp.zeros((bp,) + state_gla.shape[2:], F32)
                    x, s_fin = _gla_mix(x, q, k, v, r, g, s0, row(gla_norm[j]), w_o,
                                        batch=bp, chunk=GLA_CHUNK, tile=min(512, tp))
                else:
                    x, s_fin = _gla_mix(x, q, k, v, r, g, state_gla[j].astype(F32), row(gla_norm[j]),
                                        w_o, batch=bs, chunk=ts, tile=ts)
                gla_states[gi].append(s_fin.astype(state_gla.dtype))
            else:
                if is_prompt:
                    q, kt, vt, ktb, vtb = _sb_proj_t(
                        x, row(norm_mix[i]), w_qkv[:, :d], w_qkv[:, d:2 * d].T, w_qkv[:, 2 * d:].T,
                        batch=bp, tm=512, blk=min(SB_BLOCK, tp), q_scale=sb_hd ** -0.5)
                    o = _sb_prompt(q, ktb, vtb, sb_bias[j].astype(F32), batch=bp, hd=sb_hd)
                    k, v = (a.reshape(bp, sb_heads, sb_hd, tp).transpose(0, 3, 1, 2) for a in (kt, vt))
                else:
                    q, k, v = _row_call(
                        _sb_proj_body, [x], [row(norm_mix[i]), w_qkv],
                        [(d, BF16), (d, F32), (d, F32)], 512, d=d, q_scale=sb_hd ** -0.5)
                    o = _sb_sample(q.reshape(bs, ts, d), k.reshape(bs, ts, d), v.reshape(bs, ts, d),
                                   sb_bias[j], cache_k, cache_v, j, page_table,
                                   heads=sb_heads, hd=sb_hd).reshape(bs * ts, d)
                    k, v = (a.reshape(bs, ts, sb_heads, sb_hd) for a in (k, v))
                (x,) = _row_call(_out_proj_body, [x, o], [w_o], [(d, F32)], 1024)
                new_k[gi].append(k.astype(cache_k.dtype))
                new_v[gi].append(v.astype(cache_v.dtype))
            (x,) = _row_call(_ffn_body, [x], [row(norm_ffn2[i]), w2_in, w2_out], [(d, F32)], 512)
            (x,) = _row_call(_ple_body, [x, grp["p"][i]],
                             [row(norm_pe[i]), wg, wp, row(norm_final)], [(d, F32)], 1024,
                             final=(i == depth - 1))
            grp["x"] = x

    return (groups[0]["x"].reshape(bp, tp, d), groups[1]["x"].reshape(bs, ts, d),
            jnp.stack(gla_states[0]), jnp.stack(gla_states[1]),
            jnp.stack(new_k[0]), jnp.stack(new_v[0]), jnp.stack(new_k[1]), jnp.stack(new_v[1]))
```

```python
import functools

import jax
import jax.numpy as jnp
from jax import lax
from jax.experimental import pallas as pl
from jax.experimental.pallas import tpu as pltpu

F32 = jnp.float32
BF16 = jnp.bfloat16

NORM_EPS = 1e-6
LOG2E = 1.4426950408889634
GLA_TAU = 16.0
GLA_CHUNK = 64
SB_BLOCK = 256
PAGES_PER_STEP = 4
GLA_MAX_CHUNK_DECAY = 80.0
LANES = 128
SUBLANES = 8
VMEM_LIMIT = 56 * 1024 * 1024


def _params(*sem):
    return pltpu.CompilerParams(dimension_semantics=sem, vmem_limit_bytes=VMEM_LIMIT)


def _rms(x, g):
    return x * lax.rsqrt(jnp.mean(x * x, axis=-1, keepdims=True) + NORM_EPS) * g


def _dot(a, b):
    return jnp.dot(a, b, preferred_element_type=F32)


def _dot_nt(a, b):
    return lax.dot_general(a, b, (((1,), (1,)), ((), ())), preferred_element_type=F32)


def _dot_tn(a, b):
    return lax.dot_general(a, b, (((0,), (0,)), ((), ())), preferred_element_type=F32)


def _softplus2(z):
    neg_abs = lax.bitcast_convert_type(
        lax.bitcast_convert_type(z, jnp.uint32) | jnp.uint32(0x80000000), F32)
    return jnp.maximum(z, 0.0) + jnp.log2(1.0 + jnp.exp2(neg_abs))


def _split_bf16(x):
    hi = x.astype(BF16)
    lo = (x - hi.astype(F32)).astype(BF16)
    return hi, lo


def _const_spec(a):
    nd = a.ndim
    return pl.BlockSpec(a.shape, lambda *_: (0,) * nd)


def _row_call(body, row_args, const_args, out_dims, tm, **kw):
    n = row_args[0].shape[0]
    tm = min(tm, n)
    assert n % tm == 0
    in_specs = [pl.BlockSpec((tm, a.shape[1]), lambda i: (i, 0)) for a in row_args]
    in_specs += [_const_spec(a) for a in const_args]
    out_specs = [pl.BlockSpec((tm, d), lambda i: (i, 0)) for d, _ in out_dims]
    out_shape = [jax.ShapeDtypeStruct((n, d), dt) for d, dt in out_dims]
    outs = pl.pallas_call(
        functools.partial(body, **kw),
        grid=(n // tm,),
        in_specs=in_specs,
        out_specs=out_specs,
        out_shape=out_shape,
        compiler_params=_params("parallel"),
    )(*row_args, *const_args)
    return outs


def _ffn_body(x_ref, g_ref, win_ref, wout_ref, o_ref):
    x = x_ref[...]
    xn = _rms(x, g_ref[...]).astype(BF16)
    h = _dot(xn, win_ref[...])
    dff = wout_ref.shape[0]
    gate, up = h[:, :dff], h[:, dff:]
    act = (gate * jax.nn.sigmoid(gate) * up).astype(BF16)
    o_ref[...] = x + 0.5 * _dot(act, wout_ref[...])


def _ple_body(x_ref, p_ref, g_ref, wg_ref, wp_ref, gf_ref, o_ref, *, final):
    x = x_ref[...]
    xn = _rms(x, g_ref[...]).astype(BF16)
    gate = jax.nn.sigmoid(_dot(xn, wg_ref[...]))
    y = x + gate * _dot(p_ref[...].astype(BF16), wp_ref[...])
    if final:
        y = _rms(y, gf_ref[...])
    o_ref[...] = y


def _gla_proj_body(x_ref, g_ref, w_ref, wa_ref, wa2_ref, ba_ref,
                   q_ref, k_ref, v_ref, r_ref, gate_ref, *, dk_total, dv_total, q_scale):
    xn = _rms(x_ref[...], g_ref[...]).astype(BF16)
    y = _dot(xn, w_ref[...])
    q_ref[...] = y[:, :dk_total] * q_scale
    k_ref[...] = y[:, dk_total:2 * dk_total]
    v_ref[...] = y[:, 2 * dk_total:2 * dk_total + dv_total]
    r_ref[...] = y[:, 2 * dk_total + dv_total:]
    a = _dot(xn, wa_ref[...]).astype(BF16)
    ga = _dot(a, wa2_ref[...]) + ba_ref[...]
    gate_ref[...] = (jnp.minimum(ga, 0.0) - jnp.log1p(jnp.exp(-jnp.abs(ga)))) * (1.0 / GLA_TAU)


def _sb_proj_body(x_ref, g_ref, w_ref, q_ref, k_ref, v_ref, *, d, q_scale):
    xn = _rms(x_ref[...], g_ref[...]).astype(BF16)
    y = _dot(xn, w_ref[...])
    q_ref[...] = (y[:, :d] * q_scale).astype(BF16)
    k_ref[...] = y[:, d:2 * d]
    v_ref[...] = y[:, 2 * d:]


def _sb_proj_t_body(x_ref, g_ref, wq_ref, wkt_ref, wvt_ref, q_ref, kt_ref, vt_ref, ktb_ref, vtb_ref,
                    *, q_scale):
    xn = _rms(x_ref[...], g_ref[...]).astype(BF16)
    q_ref[...] = (_dot(xn, wq_ref[...]) * q_scale).astype(BF16)
    blk = ktb_ref.shape[3]
    for w_ref, t_ref, tb_ref in ((wkt_ref, kt_ref, ktb_ref), (wvt_ref, vt_ref, vtb_ref)):
        yt = _dot_nt(w_ref[...], xn)
        t_ref[0] = yt
        for j in range(tb_ref.shape[1]):
            tb_ref[0, j] = yt[:, j * blk:(j + 1) * blk].astype(BF16)


def _sb_proj_t(x, g, wq, wkt, wvt, *, batch, tm, blk, q_scale):
    n, d = x.shape
    seq = n // batch
    tm = min(tm, seq)
    nt = seq // tm
    row = pl.BlockSpec((tm, d), lambda b, t: (b * nt + t, 0))
    tr = pl.BlockSpec((1, d, tm), lambda b, t: (b, 0, t))
    trb = pl.BlockSpec((1, tm // blk, d, blk), lambda b, t: (b, t, 0, 0))
    return pl.pallas_call(
        functools.partial(_sb_proj_t_body, q_scale=q_scale),
        grid=(batch, nt),
        in_specs=[row, _const_spec(g), _const_spec(wq), _const_spec(wkt), _const_spec(wvt)],
        out_specs=[row, tr, tr, trb, trb],
        out_shape=[jax.ShapeDtypeStruct((n, d), BF16),
                   jax.ShapeDtypeStruct((batch, d, seq), F32),
                   jax.ShapeDtypeStruct((batch, d, seq), F32),
                   jax.ShapeDtypeStruct((batch, seq // blk, d, blk), BF16),
                   jax.ShapeDtypeStruct((batch, seq // blk, d, blk), BF16)],
        compiler_params=_params("parallel", "parallel"),
    )(x, g, wq, wkt, wvt)


def _out_proj_body(x_ref, o_ref, w_ref, y_ref):
    y_ref[...] = x_ref[...] + _dot(o_ref[...].astype(BF16), w_ref[...])


def _gla_mix_body(x_ref, q_ref, k_ref, v_ref, r_ref, g_ref, s0_ref, gn_ref, wo_ref, tri_ref,
                  y_ref, sfin_ref, st_ref, o_ref, b_ref, *, heads, chunk):
    t = pl.program_id(1)
    tile = x_ref.shape[0]
    dk = q_ref.shape[1] // heads
    dv = v_ref.shape[1] // heads

    @pl.when(t == 0)
    def _():
        for h in range(heads):
            st_ref[h] = s0_ref[0, h].T

    tri = tri_ref[...]
    causal = (lax.broadcasted_iota(jnp.int32, (chunk, chunk), 1)
              <= lax.broadcasted_iota(jnp.int32, (chunk, chunk), 0))

    def chunk_step(c, carry, *, factorable):
        rows = pl.ds(pl.multiple_of(c * chunk, chunk), chunk)
        g_hi, g_lo = _split_bf16(g_ref[rows, :])
        b = _dot(tri, g_hi) + _dot(tri, g_lo)
        q = q_ref[rows, :]
        k = k_ref[rows, :]
        v = v_ref[rows, :].astype(BF16)
        for h in range(heads):
            ks = slice(h * dk, (h + 1) * dk)
            vs = slice(h * dv, (h + 1) * dv)
            bh = b[:, ks]
            b_last = bh[chunk - 1:chunk, :]
            q_t = (q[:, ks] * jnp.exp(bh)).astype(BF16)
            k_s = (k[:, ks] * jnp.exp(b_last - bh)).astype(BF16)
            s_t = st_ref[h]
            o = _dot_nt(q_t, s_t.astype(BF16))
            if factorable:
                k_t = (k[:, ks] * jnp.exp(-bh)).astype(BF16)
                sc = jnp.where(causal, _dot_nt(q_t, k_t), 0.0)
                o = o + _dot(sc.astype(BF16), v[:, vs])
            o_ref[rows, vs] = o
            st_ref[h] = s_t * jnp.exp(b_last) + _dot_tn(v[:, vs], k_s)
        if factorable:
            return carry

        b_ref[...] = b
        s_idx = lax.broadcasted_iota(jnp.int32, (chunk, 1), 0)
        sub = lax.broadcasted_iota(jnp.int32, (SUBLANES, 1), 0)

        def token_group(t8, carry):
            rows8 = pl.ds(pl.multiple_of(c * chunk + t8 * SUBLANES, SUBLANES), SUBLANES)
            q8 = q_ref[rows8, :]
            b8 = b_ref[pl.ds(pl.multiple_of(t8 * SUBLANES, SUBLANES), SUBLANES), :]
            out = [jnp.zeros((SUBLANES, dv), F32) for _ in range(heads)]
            for r in range(SUBLANES):
                t = t8 * SUBLANES + r
                decay = jnp.exp(jnp.where(s_idx <= t, b8[r:r + 1, :] - b, -jnp.inf))
                w = q8[r:r + 1, :] * decay * k
                for h in range(heads):
                    sc = jnp.sum(w[:, h * dk:(h + 1) * dk], axis=1, keepdims=True)
                    o_t = jnp.sum(sc * v_ref[rows, h * dv:(h + 1) * dv], axis=0, keepdims=True)
                    out[h] = jnp.where(sub == r, o_t, out[h])
            for h in range(heads):
                o_ref[rows8, h * dv:(h + 1) * dv] += out[h]
            return carry

        return lax.fori_loop(0, chunk // SUBLANES, token_group, carry)

    chunk_decay = jnp.sum(g_ref[...].reshape(tile // chunk, chunk, heads * dk), axis=1)
    factorable = jnp.min(chunk_decay) >= -GLA_MAX_CHUNK_DECAY
    for flag, pred in ((True, factorable), (False, jnp.logical_not(factorable))):
        @pl.when(pred)
        def _():
            lax.fori_loop(0, tile // chunk, functools.partial(chunk_step, factorable=flag), 0)

    gn = gn_ref[...]
    r = r_ref[...]
    parts = []
    for h in range(heads):
        vs = slice(h * dv, (h + 1) * dv)
        parts.append(_rms(o_ref[:, vs], gn[:, vs]))
    on = jnp.concatenate(parts, axis=1)
    on = (on * (r * jax.nn.sigmoid(r))).astype(BF16)
    y_ref[...] = x_ref[...] + _dot(on, wo_ref[...])

    @pl.when(t == pl.num_programs(1) - 1)
    def _():
        for h in range(heads):
            sfin_ref[0, h] = st_ref[h].T


def _gla_mix(x, q, k, v, r, g, s0, gn, wo, *, batch, chunk, tile):
    n, d = x.shape
    seq = n // batch
    heads, dk, dv = s0.shape[1:]
    nt = seq // tile
    tri = jnp.tril(jnp.ones((chunk, chunk), F32)).astype(BF16)
    row = lambda w: pl.BlockSpec((tile, w), lambda b, t: (b * nt + t, 0))
    y, sfin = pl.pallas_call(
        functools.partial(_gla_mix_body, heads=heads, chunk=chunk),
        grid=(batch, nt),
        in_specs=[row(d), row(heads * dk), row(heads * dk), row(heads * dv), row(heads * dv),
                  row(heads * dk),
                  pl.BlockSpec((1, heads, dk, dv), lambda b, t: (b, 0, 0, 0)),
                  _const_spec(gn), _const_spec(wo), _const_spec(tri)],
        out_specs=[row(d), pl.BlockSpec((1, heads, dk, dv), lambda b, t: (b, 0, 0, 0))],
        out_shape=[jax.ShapeDtypeStruct((n, d), F32),
                   jax.ShapeDtypeStruct((batch, heads, dk, dv), F32)],
        scratch_shapes=[pltpu.VMEM((heads, dv, dk), F32), pltpu.VMEM((tile, heads * dv), F32),
                        pltpu.VMEM((chunk, heads * dk), F32)],
        compiler_params=_params("arbitrary", "arbitrary"),
    )(x, q, k, v, r, g, s0, gn, wo, tri)
    return y, sfin


def _sb_prompt_body(bias_ref, q_ref, k_ref, v_ref, tri_ref, o_ref,
                    z_ref, lb_ref, hi_ref, lo_ref, la_ref, *, blk, hd):
    h2 = pl.program_id(1)
    qi = pl.program_id(2)
    q2 = q_ref[0]
    tri = tri_ref[...]
    lane = lax.broadcasted_iota(jnp.int32, q2.shape, 1)
    row = lax.broadcasted_iota(jnp.int32, (blk, blk), 0)
    causal = lax.broadcasted_iota(jnp.int32, (blk, blk), 1) < row
    heads = range(q2.shape[1] // hd)
    in_head = [(lane >= i * hd) & (lane < (i + 1) * hd) for i in heads]
    qh = [jnp.where(m, q2, jnp.zeros_like(q2)) for m in in_head]
    bias2 = [bias_ref[h2 * len(heads) + i] * LOG2E for i in heads]

    def step(carry, back=None, mid=False, soft=False, mask=None, front=None):
        accs, runs = list(carry[0::2]), list(carry[1::2])
        if back is not None:
            vt = v_ref[0, back]
            for i in heads:
                accs[i] = accs[i] + _dot_nt(jnp.exp2(la_ref[i]).astype(BF16), vt)
        if mid:
            sp = [jnp.concatenate([hi_ref[i], lo_ref[i]], axis=1) for i in heads]
            later = [_dot(sp[i], tri) for i in heads]
        if front is not None:
            kt = k_ref[0, front]
            z = [_dot(qh[i], kt) for i in heads]
        if mid:
            for i in heads:
                la_ref[i] = lb_ref[i] - later[i] + runs[i]
                runs[i] = runs[i] - (later[i][:, 0:1] + sp[i][:, 0:1].astype(F32)
                                     + sp[i][:, blk:blk + 1].astype(F32))
        if soft:
            for i in heads:
                z2 = z_ref[i]
                s = _softplus2(z2)
                lb = z2 - s
                if mask is not None:
                    s = jnp.where(mask, s, 0.0)
                    lb = jnp.where(mask, lb, -jnp.inf)
                hi_ref[i], lo_ref[i] = _split_bf16(s)
                lb_ref[i] = lb
        if front is not None:
            for i in heads:
                z_ref[i] = z[i] * LOG2E + bias2[i]
        return tuple(x for pair in zip(accs, runs) for x in pair)

    blk_at = lambda t: jnp.maximum(qi - t, 0)
    valid = lambda t: row < jnp.where(qi >= t, blk, 0)
    carry = (jnp.zeros(q2.shape, F32), jnp.zeros((blk, 1), F32)) * len(heads)
    carry = step(carry, front=qi)
    carry = step(carry, soft=True, mask=causal, front=blk_at(1))
    carry = step(carry, mid=True, soft=True, mask=valid(1), front=blk_at(2))
    carry = lax.fori_loop(
        0, qi - 2, lambda j, c: step(c, back=qi - j, mid=True, soft=True, front=qi - 3 - j), carry)
    last = jnp.minimum(qi, 2)
    carry = step(carry, back=last, mid=True, soft=True, mask=valid(2))
    carry = step(carry, back=jnp.maximum(last - 1, 0), mid=True)
    carry = step(carry, back=0)
    o = carry[0]
    for i in heads[1:]:
        o = jnp.where(in_head[i], carry[2 * i], o)
    o_ref[0] = o.astype(o_ref.dtype)


def _sb_prompt(q, kt, vt, bias, *, batch, hd):
    n, d = q.shape
    seq = n // batch
    nq, blk = kt.shape[1], kt.shape[3]
    q3 = q.reshape(batch, seq, d)
    tri = jnp.tile((jnp.arange(blk)[:, None] > jnp.arange(blk)[None, :]).astype(BF16), (2, 1))
    kv_spec = pl.BlockSpec((1, nq, LANES, blk), lambda b, h, i: (b, 0, h, 0))
    stage = lambda dt: pltpu.VMEM((LANES // hd, blk, blk), dt)
    o = pl.pallas_call(
        functools.partial(_sb_prompt_body, blk=blk, hd=hd),
        grid=(batch, d // LANES, nq),
        in_specs=[pl.BlockSpec(memory_space=pltpu.SMEM),
                  pl.BlockSpec((1, blk, LANES), lambda b, h, i: (b, i, h)),
                  kv_spec, kv_spec, _const_spec(tri)],
        out_specs=pl.BlockSpec((1, blk, LANES), lambda b, h, i: (b, i, h)),
        out_shape=jax.ShapeDtypeStruct((batch, seq, d), BF16),
        scratch_shapes=[stage(F32), stage(F32), stage(BF16), stage(BF16), stage(F32)],
        compiler_params=_params("parallel", "parallel", "arbitrary"),
    )(bias, q3, kt, vt, tri)
    return o.reshape(n, d)


def _sb_sample_body(pt_ref, qbd_ref, bias_ref, kn_ref, vn_ref, tri_ref, *rest, pages, heads, hd):
    k_refs = rest[:pages]
    v_refs = rest[pages:2 * pages]
    o_ref, acc_ref, run_ref, kpad_ref, vpad_ref = rest[2 * pages:]
    p = pl.program_id(1)
    t_new = kn_ref.shape[1]
    rows = qbd_ref.shape[1]
    psz = kpad_ref.shape[0]
    qbd = qbd_ref[0]
    bias = bias_ref[...]
    tri = tri_ref[...]

    def sweep(kps, vps, causal):
        z = [((_dot(qbd, kp) if causal is None else _dot_nt(qbd, kp)) + bias) * LOG2E for kp in kps]
        sp = [_softplus2(zi) for zi in z]
        lb = [zi - si for zi, si in zip(z, sp)]
        if causal is not None:
            sp = [jnp.where(causal, si, 0.0) for si in sp]
            lb = [jnp.where(causal, li, -jnp.inf) for li in lb]
        both = [_dot(jnp.concatenate(_split_bf16(si), axis=1), tri) for si in sp]
        run = run_ref[...]
        acc = acc_ref[...]
        for li, bi, vp in zip(lb, both, vps):
            a = jnp.exp2(li - bi[:, :psz] + run).astype(BF16)
            run = run - bi[:, psz:]
            acc = acc + (_dot_nt(a, vp) if causal is None else _dot(a, vp))
        run_ref[...] = run
        acc_ref[...] = acc

    @pl.when(p == 0)
    def _():
        acc_ref[...] = jnp.zeros_like(acc_ref)
        run_ref[...] = jnp.zeros_like(run_ref)
        kpad_ref[...] = jnp.zeros_like(kpad_ref)
        vpad_ref[...] = jnp.zeros_like(vpad_ref)
        kpad_ref[0:t_new, :] = kn_ref[0]
        vpad_ref[0:t_new, :] = vn_ref[0]
        tq = lax.broadcasted_iota(jnp.int32, (rows, psz), 0) % t_new
        causal = lax.broadcasted_iota(jnp.int32, (rows, psz), 1) < tq
        sweep([kpad_ref[...].astype(BF16)], [vpad_ref[...].astype(BF16)], causal)

    sweep([r[0].astype(BF16) for r in k_refs], [r[0].astype(BF16) for r in v_refs], None)

    @pl.when(p == pl.num_programs(1) - 1)
    def _():
        lane = lax.broadcasted_iota(jnp.int32, (t_new, LANES), 1)
        per_tile = LANES // hd
        for gidx in range(heads // per_tile):
            cols = slice(gidx * LANES, (gidx + 1) * LANES)
            o = acc_ref[gidx * per_tile * t_new:(gidx * per_tile + 1) * t_new, cols]
            for i in range(1, per_tile):
                h = gidx * per_tile + i
                o = jnp.where(lane >= i * hd, acc_ref[h * t_new:(h + 1) * t_new, cols], o)
            o_ref[0, :, cols] = o


def _sb_sample(q, k_new, v_new, bias, cache_k, cache_v, layer, page_table, *, heads, hd):
    db, t_new, d = q.shape
    n_layers, n_phys, psz = cache_k.shape[:3]
    n_pages = page_table.shape[1]
    pages = PAGES_PER_STEP if n_pages % PAGES_PER_STEP == 0 else 1
    ck = cache_k.transpose(0, 1, 3, 4, 2).reshape(n_layers * n_phys, d, psz)
    cv = cache_v.transpose(0, 1, 3, 4, 2).reshape(n_layers * n_phys, d, psz)
    head_of_lane = jnp.arange(d) // hd
    qbd = jnp.where(head_of_lane[None, None, None, :] == jnp.arange(heads)[None, :, None, None],
                    q[:, None, :, :], jnp.zeros((), q.dtype)).reshape(db, heads * t_new, d)
    bias_rows = jnp.broadcast_to(jnp.repeat(bias.astype(F32), t_new)[:, None], (heads * t_new, psz))
    tri = jnp.concatenate([(jnp.arange(psz)[:, None] > jnp.arange(psz)[None, :]).astype(BF16),
                           jnp.ones((psz, psz), BF16)], axis=1)
    tri = jnp.tile(tri, (2, 1))
    base = layer * n_phys

    def page_spec(i):
        return pl.BlockSpec(
            (1, d, psz), lambda b, p, pt: (pt[b, n_pages - 1 - (p * pages + i)] + base, 0, 0))

    per_seq = lambda shape: pl.BlockSpec(shape, lambda b, p, pt: (b, 0, 0))
    const = lambda a: pl.BlockSpec(a.shape, lambda b, p, pt: (0,) * a.ndim)
    grid_spec = pltpu.PrefetchScalarGridSpec(
        num_scalar_prefetch=1,
        grid=(db, n_pages // pages),
        in_specs=[per_seq((1, heads * t_new, d)), const(bias_rows),
                  per_seq((1, t_new, d)), per_seq((1, t_new, d)), const(tri)]
                 + [page_spec(i) for i in range(pages)] * 2,
        out_specs=per_seq((1, t_new, d)),
        scratch_shapes=[pltpu.VMEM((heads * t_new, d), F32), pltpu.VMEM((heads * t_new, psz), F32),
                        pltpu.VMEM((psz, d), F32), pltpu.VMEM((psz, d), F32)],
    )
    return pl.pallas_call(
        functools.partial(_sb_sample_body, pages=pages, heads=heads, hd=hd),
        grid_spec=grid_spec,
        out_shape=jax.ShapeDtypeStruct((db, t_new, d), F32),
        compiler_params=_params("parallel", "arbitrary"),
    )(page_table, qbd, bias_rows, k_new, v_new, tri, *([ck] * pages), *([cv] * pages))


def kernel(x_prompt, x_sample, state_gla, cache_k, cache_v, page_table, p_prompt, p_sample,
           norm_ffn1, ffn1_w_in, ffn1_w_out, norm_mix, gla_w_in, gla_w_a2, gla_b_a, gla_norm,
           gla_w_out, sb_w_qkv, sb_bias, sb_w_out, norm_ffn2, ffn2_w_in, ffn2_w_out, norm_pe,
           pe_w_gate, pe_w_proj, norm_final):
    bp, tp, d = x_prompt.shape
    bs, ts, _ = x_sample.shape
    depth = norm_ffn1.shape[0]
    n_mixers = 2
    gla_heads, gla_dk, gla_dv = state_gla.shape[2:]
    dk_total, dv_total = gla_heads * gla_dk, gla_heads * gla_dv
    rank = gla_w_a2.shape[1]
    sb_heads, sb_hd = cache_k.shape[3:]
    row = lambda a: a.reshape(1, -1)

    groups = [dict(x=x_prompt.reshape(bp * tp, d), p=p_prompt.reshape(depth, bp * tp, -1), batch=bp),
              dict(x=x_sample.reshape(bs * ts, d), p=p_sample.reshape(depth, bs * ts, -1), batch=bs)]
    gla_states = [[], []]
    new_k = [[], []]
    new_v = [[], []]

    for i in range(depth):
        j = i // n_mixers
        w1_in, w1_out = ffn1_w_in[i].astype(BF16), ffn1_w_out[i].astype(BF16)
        w2_in, w2_out = ffn2_w_in[i].astype(BF16), ffn2_w_out[i].astype(BF16)
        wg, wp = pe_w_gate[i].astype(BF16), pe_w_proj[i].astype(BF16)
        if i % n_mixers == 0:
            w_main = gla_w_in[j, :, :2 * dk_total + 2 * dv_total].astype(BF16)
            w_a = jnp.pad(gla_w_in[j, :, 2 * dk_total + 2 * dv_total:],
                          ((0, 0), (0, LANES - rank))).astype(BF16)
            w_a2 = jnp.pad(gla_w_a2[j], ((0, LANES - rank), (0, 0))).astype(BF16)
            w_o = gla_w_out[j].astype(BF16)
        else:
            w_qkv = sb_w_qkv[j].astype(BF16)
            w_o = sb_w_out[j].astype(BF16)

        for gi, grp in enumerate(groups):
            x = grp["x"]
            is_prompt = gi == 0
            (x,) = _row_call(_ffn_body, [x], [row(norm_ffn1[i]), w1_in, w1_out], [(d, F32)], 512)
            if i % n_mixers == 0:
                q, k, v, r, g = _row_call(
                    _gla_proj_body, [x], [row(norm_mix[i]), w_main, w_a, w_a2, row(gla_b_a[j])],
                    [(dk_total, F32), (dk_total, F32), (dv_total, F32), (dv_total, F32), (dk_total, F32)],
                    512, dk_total=dk_total, dv_total=dv_total, q_scale=gla_dk ** -0.5)
                if is_prompt:
                    s0 = jnp.zeros((bp,) + state_gla.shape[2:], F32)
                    x, s_fin = _gla_mix(x, q, k, v, r, g, s0, row(gla_norm[j]), w_o,
                                        batch=bp, chunk=GLA_CHUNK, tile=min(512, tp))
                else:
                    x, s_fin = _gla_mix(x, q, k, v, r, g, state_gla[j].astype(F32), row(gla_norm[j]),
                                        w_o, batch=bs, chunk=ts, tile=ts)
                gla_states[gi].append(s_fin.astype(state_gla.dtype))
            else:
                if is_prompt:
                    q, kt, vt, ktb, vtb = _sb_proj_t(
                        x, row(norm_mix[i]), w_qkv[:, :d], w_qkv[:, d:2 * d].T, w_qkv[:, 2 * d:].T,
                        batch=bp, tm=512, blk=min(SB_BLOCK, tp), q_scale=sb_hd ** -0.5)
                    o = _sb_prompt(q, ktb, vtb, sb_bias[j].astype(F32), batch=bp, hd=sb_hd)
                    k, v = (a.reshape(bp, sb_heads, sb_hd, tp).transpose(0, 3, 1, 2) for a in (kt, vt))
                else:
                    q, k, v = _row_call(
                        _sb_proj_body, [x], [row(norm_mix[i]), w_qkv],
                        [(d, BF16), (d, F32), (d, F32)], 512, d=d, q_scale=sb_hd ** -0.5)
                    o = _sb_sample(q.reshape(bs, ts, d), k.reshape(bs, ts, d), v.reshape(bs, ts, d),
                                   sb_bias[j], cache_k, cache_v, j, page_table,
                                   heads=sb_heads, hd=sb_hd).reshape(bs * ts, d)
                    k, v = (a.reshape(bs, ts, sb_heads, sb_hd) for a in (k, v))
                (x,) = _row_call(_out_proj_body, [x, o], [w_o], [(d, F32)], 1024)
                new_k[gi].append(k.astype(cache_k.dtype))
                new_v[gi].append(v.astype(cache_v.dtype))
            (x,) = _row_call(_ffn_body, [x], [row(norm_ffn2[i]), w2_in, w2_out], [(d, F32)], 512)
            (x,) = _row_call(_ple_body, [x, grp["p"][i]],
                             [row(norm_pe[i]), wg, wp, row(norm_final)], [(d, F32)], 1024,
                             final=(i == depth - 1))
            grp["x"] = x

    return (groups[0]["x"].reshape(bp, tp, d), groups[1]["x"].reshape(bs, ts, d),
            jnp.stack(gla_states[0]), jnp.stack(gla_states[1]),
            jnp.stack(new_k[0]), jnp.stack(new_v[0]), jnp.stack(new_k[1]), jnp.stack(new_v[1]))
```

```python
import functools

import jax
import jax.numpy as jnp
from jax import lax
from jax.experimental import pallas as pl
from jax.experimental.pallas import tpu as pltpu

F32 = jnp.float32
BF16 = jnp.bfloat16

NORM_EPS = 1e-6
LOG2E = 1.4426950408889634
GLA_TAU = 16.0
GLA_CHUNK = 64
SB_BLOCK = 256
PAGES_PER_STEP = 8
GLA_MAX_CHUNK_DECAY = 80.0
LANES = 128
SUBLANES = 8
VMEM_LIMIT = 56 * 1024 * 1024


def _params(*sem):
    return pltpu.CompilerParams(dimension_semantics=sem, vmem_limit_bytes=VMEM_LIMIT)


def _rms(x, g):
    return x * lax.rsqrt(jnp.mean(x * x, axis=-1, keepdims=True) + NORM_EPS) * g


def _dot(a, b):
    return jnp.dot(a, b, preferred_element_type=F32)


def _dot_nt(a, b):
    return lax.dot_general(a, b, (((1,), (1,)), ((), ())), preferred_element_type=F32)


def _dot_tn(a, b):
    return lax.dot_general(a, b, (((0,), (0,)), ((), ())), preferred_element_type=F32)


def _softplus(z):
    neg_abs = lax.bitcast_convert_type(
        lax.bitcast_convert_type(z, jnp.uint32) | jnp.uint32(0x80000000), F32)
    return jnp.maximum(z, 0.0) + jnp.log(1.0 + jnp.exp(neg_abs))


def _split_bf16(x):
    hi = x.astype(BF16)
    lo = (x - hi.astype(F32)).astype(BF16)
    return hi, lo


def _const_spec(a):
    nd = a.ndim
    return pl.BlockSpec(a.shape, lambda *_: (0,) * nd)


def _row_call(body, row_args, const_args, out_dims, tm, **kw):
    n = row_args[0].shape[0]
    tm = min(tm, n)
    assert n % tm == 0
    in_specs = [pl.BlockSpec((tm, a.shape[1]), lambda i: (i, 0)) for a in row_args]
    in_specs += [_const_spec(a) for a in const_args]
    out_specs = [pl.BlockSpec((tm, d), lambda i: (i, 0)) for d, _ in out_dims]
    out_shape = [jax.ShapeDtypeStruct((n, d), dt) for d, dt in out_dims]
    outs = pl.pallas_call(
        functools.partial(body, **kw),
        grid=(n // tm,),
        in_specs=in_specs,
        out_specs=out_specs,
        out_shape=out_shape,
        compiler_params=_params("parallel"),
    )(*row_args, *const_args)
    return outs


def _ffn_body(x_ref, g_ref, win_ref, wout_ref, o_ref):
    x = x_ref[...]
    xn = _rms(x, g_ref[...]).astype(BF16)
    h = _dot(xn, win_ref[...])
    dff = wout_ref.shape[0]
    gate, up = h[:, :dff], h[:, dff:]
    act = (gate * jax.nn.sigmoid(gate) * up).astype(BF16)
    o_ref[...] = x + 0.5 * _dot(act, wout_ref[...])


def _ple_body(x_ref, p_ref, g_ref, wg_ref, wp_ref, gf_ref, o_ref, *, final):
    x = x_ref[...]
    xn = _rms(x, g_ref[...]).astype(BF16)
    gate = jax.nn.sigmoid(_dot(xn, wg_ref[...]))
    y = x + gate * _dot(p_ref[...].astype(BF16), wp_ref[...])
    if final:
        y = _rms(y, gf_ref[...])
    o_ref[...] = y


def _gla_proj_body(x_ref, g_ref, w_ref, wa_ref, wa2_ref, ba_ref,
                   q_ref, k_ref, v_ref, r_ref, gate_ref, *, dk_total, dv_total, q_scale):
    xn = _rms(x_ref[...], g_ref[...]).astype(BF16)
    y = _dot(xn, w_ref[...])
    q_ref[...] = y[:, :dk_total] * q_scale
    k_ref[...] = y[:, dk_total:2 * dk_total]
    v_ref[...] = y[:, 2 * dk_total:2 * dk_total + dv_total]
    r_ref[...] = y[:, 2 * dk_total + dv_total:]
    a = _dot(xn, wa_ref[...]).astype(BF16)
    ga = _dot(a, wa2_ref[...]) + ba_ref[...]
    gate_ref[...] = (jnp.minimum(ga, 0.0) - jnp.log1p(jnp.exp(-jnp.abs(ga)))) * (1.0 / GLA_TAU)


def _sb_proj_body(x_ref, g_ref, w_ref, q_ref, k_ref, v_ref, *, d, q_scale):
    xn = _rms(x_ref[...], g_ref[...]).astype(BF16)
    y = _dot(xn, w_ref[...])
    q_ref[...] = (y[:, :d] * q_scale).astype(BF16)
    k_ref[...] = y[:, d:2 * d]
    v_ref[...] = y[:, 2 * d:]


def _sb_proj_t_body(x_ref, g_ref, wq_ref, wkt_ref, wvt_ref, q_ref, kt_ref, vt_ref, ktb_ref, vtb_ref,
                    *, q_scale):
    xn = _rms(x_ref[...], g_ref[...]).astype(BF16)
    q_ref[...] = (_dot(xn, wq_ref[...]) * q_scale).astype(BF16)
    blk = ktb_ref.shape[3]
    for w_ref, t_ref, tb_ref in ((wkt_ref, kt_ref, ktb_ref), (wvt_ref, vt_ref, vtb_ref)):
        yt = _dot_nt(w_ref[...], xn)
        t_ref[0] = yt
        for j in range(tb_ref.shape[1]):
            tb_ref[0, j] = yt[:, j * blk:(j + 1) * blk].astype(BF16)


def _sb_proj_t(x, g, wq, wkt, wvt, *, batch, tm, blk, q_scale):
    n, d = x.shape
    seq = n // batch
    tm = min(tm, seq)
    nt = seq // tm
    row = pl.BlockSpec((tm, d), lambda b, t: (b * nt + t, 0))
    tr = pl.BlockSpec((1, d, tm), lambda b, t: (b, 0, t))
    trb = pl.BlockSpec((1, tm // blk, d, blk), lambda b, t: (b, t, 0, 0))
    return pl.pallas_call(
        functools.partial(_sb_proj_t_body, q_scale=q_scale),
        grid=(batch, nt),
        in_specs=[row, _const_spec(g), _const_spec(wq), _const_spec(wkt), _const_spec(wvt)],
        out_specs=[row, tr, tr, trb, trb],
        out_shape=[jax.ShapeDtypeStruct((n, d), BF16),
                   jax.ShapeDtypeStruct((batch, d, seq), F32),
                   jax.ShapeDtypeStruct((batch, d, seq), F32),
                   jax.ShapeDtypeStruct((batch, seq // blk, d, blk), BF16),
                   jax.ShapeDtypeStruct((batch, seq // blk, d, blk), BF16)],
        compiler_params=_params("parallel", "parallel"),
    )(x, g, wq, wkt, wvt)


def _out_proj_body(x_ref, o_ref, w_ref, y_ref):
    y_ref[...] = x_ref[...] + _dot(o_ref[...].astype(BF16), w_ref[...])


def _gla_mix_body(x_ref, q_ref, k_ref, v_ref, r_ref, g_ref, s0_ref, gn_ref, wo_ref, tri_ref,
                  y_ref, sfin_ref, st_ref, o_ref, b_ref, qt_ref, kt_ref, ks_ref, dec_ref,
                  *, heads, chunk):
    t = pl.program_id(1)
    tile = x_ref.shape[0]
    dk = q_ref.shape[1] // heads
    dv = v_ref.shape[1] // heads

    @pl.when(t == 0)
    def _():
        for h in range(heads):
            st_ref[h] = s0_ref[0, h].T

    tri = tri_ref[...]
    causal = (lax.broadcasted_iota(jnp.int32, (chunk, chunk), 1)
              <= lax.broadcasted_iota(jnp.int32, (chunk, chunk), 0))

    n_chunks = tile // chunk
    kd = heads * dk

    for c in range(n_chunks):
        rows = slice(c * chunk, (c + 1) * chunk)
        g_hi, g_lo = _split_bf16(g_ref[rows, :])
        b_ref[rows, :] = _dot(tri, g_hi) + _dot(tri, g_lo)
    b3 = b_ref[...].reshape(n_chunks, chunk, kd)
    b_last = b3[:, chunk - 1:chunk, :]
    q3 = q_ref[...].reshape(n_chunks, chunk, kd)
    k3 = k_ref[...].reshape(n_chunks, chunk, kd)
    qt_ref[...] = (q3 * jnp.exp(b3)).astype(BF16).reshape(tile, kd)
    ks_ref[...] = (k3 * jnp.exp(b_last - b3)).astype(BF16).reshape(tile, kd)
    dec_ref[...] = jnp.broadcast_to(jnp.exp(b_last), (n_chunks, SUBLANES, kd))
    factorable = jnp.min(b_last) >= -GLA_MAX_CHUNK_DECAY

    def chunk_step(c, carry, *, factorable):
        rows = pl.ds(pl.multiple_of(c * chunk, chunk), chunk)
        q_t = qt_ref[rows, :]
        k_s = ks_ref[rows, :]
        v = v_ref[rows, :].astype(BF16)
        s_t = [st_ref[h] for h in range(heads)]
        hk = [slice(h * dk, (h + 1) * dk) for h in range(heads)]
        hv = [slice(h * dv, (h + 1) * dv) for h in range(heads)]
        o = [_dot_nt(q_t[:, hk[h]], s_t[h].astype(BF16)) for h in range(heads)]
        if factorable:
            k_t = kt_ref[rows, :]
            sc = [jnp.where(causal, _dot_nt(q_t[:, hk[h]], k_t[:, hk[h]]), 0.0).astype(BF16)
                  for h in range(heads)]
        upd = [_dot_tn(v[:, hv[h]], k_s[:, hk[h]]) for h in range(heads)]
        if factorable:
            o = [o[h] + _dot(sc[h], v[:, hv[h]]) for h in range(heads)]
        dec = dec_ref[c]
        for h in range(heads):
            o_ref[rows, hv[h]] = o[h]
            st_ref[h] = s_t[h] * dec[0:1, hk[h]] + upd[h]
        if factorable:
            return carry

        b = b_ref[rows, :]
        k = k_ref[rows, :]
        s_idx = lax.broadcasted_iota(jnp.int32, (chunk, 1), 0)
        sub = lax.broadcasted_iota(jnp.int32, (SUBLANES, 1), 0)

        def token_group(t8, carry):
            rows8 = pl.ds(pl.multiple_of(c * chunk + t8 * SUBLANES, SUBLANES), SUBLANES)
            q8 = q_ref[rows8, :]
            b8 = b_ref[rows8, :]
            out = [jnp.zeros((SUBLANES, dv), F32) for _ in range(heads)]
            for r in range(SUBLANES):
                t = t8 * SUBLANES + r
                decay = jnp.exp(jnp.where(s_idx <= t, b8[r:r + 1, :] - b, -jnp.inf))
                w = q8[r:r + 1, :] * decay * k
                for h in range(heads):
                    sc = jnp.sum(w[:, h * dk:(h + 1) * dk], axis=1, keepdims=True)
                    o_t = jnp.sum(sc * v_ref[rows, h * dv:(h + 1) * dv], axis=0, keepdims=True)
                    out[h] = jnp.where(sub == r, o_t, out[h])
            for h in range(heads):
                o_ref[rows8, h * dv:(h + 1) * dv] += out[h]
            return carry

        return lax.fori_loop(0, chunk // SUBLANES, token_group, carry)

    for flag, pred in ((True, factorable), (False, jnp.logical_not(factorable))):
        @pl.when(pred)
        def _():
            if flag:
                kt_ref[...] = (k3 * jnp.exp(-b3)).astype(BF16).reshape(tile, kd)
            lax.fori_loop(0, n_chunks, functools.partial(chunk_step, factorable=flag), 0)

    gn = gn_ref[...]
    r = r_ref[...]
    parts = []
    for h in range(heads):
        vs = slice(h * dv, (h + 1) * dv)
        parts.append(_rms(o_ref[:, vs], gn[:, vs]))
    on = jnp.concatenate(parts, axis=1)
    on = (on * (r * jax.nn.sigmoid(r))).astype(BF16)
    y_ref[...] = x_ref[...] + _dot(on, wo_ref[...])

    @pl.when(t == pl.num_programs(1) - 1)
    def _():
        for h in range(heads):
            sfin_ref[0, h] = st_ref[h].T


def _gla_mix(x, q, k, v, r, g, s0, gn, wo, *, batch, chunk, tile):
    n, d = x.shape
    seq = n // batch
    heads, dk, dv = s0.shape[1:]
    nt = seq // tile
    tri = jnp.tril(jnp.ones((chunk, chunk), F32)).astype(BF16)
    row = lambda w: pl.BlockSpec((tile, w), lambda b, t: (b * nt + t, 0))
    y, sfin = pl.pallas_call(
        functools.partial(_gla_mix_body, heads=heads, chunk=chunk),
        grid=(batch, nt),
        in_specs=[row(d), row(heads * dk), row(heads * dk), row(heads * dv), row(heads * dv),
                  row(heads * dk),
                  pl.BlockSpec((1, heads, dk, dv), lambda b, t: (b, 0, 0, 0)),
                  _const_spec(gn), _const_spec(wo), _const_spec(tri)],
        out_specs=[row(d), pl.BlockSpec((1, heads, dk, dv), lambda b, t: (b, 0, 0, 0))],
        out_shape=[jax.ShapeDtypeStruct((n, d), F32),
                   jax.ShapeDtypeStruct((batch, heads, dk, dv), F32)],
        scratch_shapes=[pltpu.VMEM((heads, dv, dk), F32), pltpu.VMEM((tile, heads * dv), F32),
                        pltpu.VMEM((tile, heads * dk), F32)]
                       + [pltpu.VMEM((tile, heads * dk), BF16)] * 3
                       + [pltpu.VMEM((tile // chunk, SUBLANES, heads * dk), F32)],
        compiler_params=_params("arbitrary", "arbitrary"),
    )(x, q, k, v, r, g, s0, gn, wo, tri)
    return y, sfin


def _sb_prompt_body(bias_ref, q_ref, k_ref, v_ref, tri_ref, o_ref,
                    z_ref, lb_ref, hi_ref, la_ref, zb_ref, qh_ref, *, blk, hd):
    h2 = pl.program_id(1)
    nq = k_ref.shape[1]
    n_pairs = nq * (nq + 1) // 2
    tri = tri_ref[...]
    lane = lax.broadcasted_iota(jnp.int32, (blk, LANES), 1)
    row = lax.broadcasted_iota(jnp.int32, (blk, blk), 0)
    col = lax.broadcasted_iota(jnp.int32, (blk, blk), 1)
    heads = range(LANES // hd)
    in_head = [(lane >= i * hd) & (lane < (i + 1) * hd) for i in heads]

    q_all = q_ref[0]
    q_lane = lax.broadcasted_iota(jnp.int32, q_all.shape, 1)
    for i in heads:
        b = bias_ref[h2 * len(heads) + i]
        zb_ref[i, 0] = jnp.full((blk, blk), b, F32)
        zb_ref[i, 1] = jnp.where(col < row, b, -jnp.inf)
        qh_ref[i] = jnp.where((q_lane >= i * hd) & (q_lane < (i + 1) * hd), q_all,
                              jnp.zeros_like(q_all))
    z_ref[...] = jnp.zeros_like(z_ref)
    lb_ref[...] = jnp.zeros_like(lb_ref)
    hi_ref[...] = jnp.zeros_like(hi_ref)
    la_ref[...] = jnp.zeros_like(la_ref)

    def following(qi, kb):
        step_q = jnp.minimum(qi + 1, nq - 1)
        return jnp.where(kb > 0, qi, step_q), jnp.where(kb > 0, kb - 1, step_q)

    def body(carry, slot):
        accs, runs, pairs = list(carry[0]), list(carry[1]), carry[2]
        (qf, kf), (qs, ks), (qm, km), (qb, kb) = pairs
        vt = v_ref[0, kb]
        for i in heads:
            a = jnp.exp(la_ref[i]).astype(BF16)
            accs[i] = jnp.where(kb == qb, 0.0, accs[i]) + _dot_nt(a, vt)
        q_rows = pl.ds(pl.multiple_of(qf * blk, blk), blk)
        kt = k_ref[0, kf]
        diag = (kf == qf).astype(jnp.int32)
        for i in heads:
            z_ref[i, 1 - slot] = _dot(qh_ref[i, q_rows, :], kt) + zb_ref[i, diag]
        for i in heads:
            hi = hi_ref[i]
            later = _dot(hi, tri)
            run = jnp.where(km == qm, 0.0, runs[i])
            la_ref[i] = lb_ref[i] - later + run
            runs[i] = run - (later[:, 0:1] + hi[:, 0:1].astype(F32))
        for i in heads:
            zi = z_ref[i, slot]
            s = _softplus(zi)
            lb_ref[i] = zi - s
            hi_ref[i] = s.astype(BF16)
        o = accs[0]
        for i in heads[1:]:
            o = jnp.where(in_head[i], accs[i], o)
        o_ref[0, pl.ds(pl.multiple_of(qb * blk, blk), blk), :] = o.astype(o_ref.dtype)
        return tuple(accs), tuple(runs), (following(qf, kf), (qf, kf), (qs, ks), (qm, km))

    zero = jnp.int32(0)
    carry = (tuple(jnp.zeros((blk, LANES), F32) for _ in heads),
             tuple(jnp.zeros((blk, 1), F32) for _ in heads),
             ((zero, zero),) * 4)
    steps = n_pairs + 3
    carry = lax.fori_loop(0, steps // 2, lambda _, c: body(body(c, 0), 1), carry)
    if steps % 2:
        body(carry, 0)


def _sb_prompt(q, kt, vt, bias, *, batch, hd):
    n, d = q.shape
    seq = n // batch
    nq, blk = kt.shape[1], kt.shape[3]
    q3 = q.reshape(batch, seq, d)
    tri = (jnp.arange(blk)[:, None] > jnp.arange(blk)[None, :]).astype(BF16)
    heads = LANES // hd
    q_spec = pl.BlockSpec((1, seq, LANES), lambda b, h: (b, 0, h))
    kv_spec = pl.BlockSpec((1, nq, LANES, blk), lambda b, h: (b, 0, h, 0))
    stage = lambda dt: pltpu.VMEM((heads, blk, blk), dt)
    o = pl.pallas_call(
        functools.partial(_sb_prompt_body, blk=blk, hd=hd),
        grid=(batch, d // LANES),
        in_specs=[pl.BlockSpec(memory_space=pltpu.SMEM), q_spec, kv_spec, kv_spec, _const_spec(tri)],
        out_specs=q_spec,
        out_shape=jax.ShapeDtypeStruct((batch, seq, d), BF16),
        scratch_shapes=[pltpu.VMEM((heads, 2, blk, blk), F32), stage(F32), stage(BF16), stage(F32),
                        pltpu.VMEM((heads, 2, blk, blk), F32),
                        pltpu.VMEM((heads, seq, LANES), BF16)],
        compiler_params=_params("parallel", "parallel"),
    )(bias, q3, kt, vt, tri)
    return o.reshape(n, d)


def _sb_sample_body(pt_ref, qbd_ref, bias_ref, kn_ref, vn_ref, tri_ref, *rest, pages, heads, hd):
    k_refs = rest[:pages]
    v_refs = rest[pages:2 * pages]
    o_ref, acc_ref, run_ref, kpad_ref, vpad_ref = rest[2 * pages:]
    p = pl.program_id(1)
    t_new = kn_ref.shape[1]
    rows = qbd_ref.shape[1]
    psz = kpad_ref.shape[0]
    qbd = qbd_ref[0]
    bias = bias_ref[...]
    tri = tri_ref[...]

    def sweep(kps, vps, causal):
        z = [(_dot(qbd, kp) if causal is None else _dot_nt(qbd, kp)) + bias for kp in kps]
        sp = [_softplus(zi) for zi in z]
        lb = [zi - si for zi, si in zip(z, sp)]
        if causal is not None:
            sp = [jnp.where(causal, si, 0.0) for si in sp]
            lb = [jnp.where(causal, li, -jnp.inf) for li in lb]
        both = [_dot(jnp.concatenate(_split_bf16(si), axis=1), tri) for si in sp]
        run = run_ref[...]
        acc = acc_ref[...]
        for li, bi, vp in zip(lb, both, vps):
            a = jnp.exp(li - bi[:, :psz] + run).astype(BF16)
            run = run - bi[:, psz:]
            acc = acc + (_dot_nt(a, vp) if causal is None else _dot(a, vp))
        run_ref[...] = run
        acc_ref[...] = acc

    @pl.when(p == 0)
    def _():
        acc_ref[...] = jnp.zeros_like(acc_ref)
        run_ref[...] = jnp.zeros_like(run_ref)
        kpad_ref[...] = jnp.zeros_like(kpad_ref)
        vpad_ref[...] = jnp.zeros_like(vpad_ref)
        kpad_ref[0:t_new, :] = kn_ref[0]
        vpad_ref[0:t_new, :] = vn_ref[0]
        tq = lax.broadcasted_iota(jnp.int32, (rows, psz), 0) % t_new
        causal = lax.broadcasted_iota(jnp.int32, (rows, psz), 1) < tq
        sweep([kpad_ref[...].astype(BF16)], [vpad_ref[...].astype(BF16)], causal)

    sweep([r[0].astype(BF16) for r in k_refs], [r[0].astype(BF16) for r in v_refs], None)

    @pl.when(p == pl.num_programs(1) - 1)
    def _():
        lane = lax.broadcasted_iota(jnp.int32, (t_new, LANES), 1)
        per_tile = LANES // hd
        for gidx in range(heads // per_tile):
            cols = slice(gidx * LANES, (gidx + 1) * LANES)
            o = acc_ref[gidx * per_tile * t_new:(gidx * per_tile + 1) * t_new, cols]
            for i in range(1, per_tile):
                h = gidx * per_tile + i
                o = jnp.where(lane >= i * hd, acc_ref[h * t_new:(h + 1) * t_new, cols], o)
            o_ref[0, :, cols] = o


def _sb_sample(q, k_new, v_new, bias, cache_k, cache_v, layer, page_table, *, heads, hd):
    db, t_new, d = q.shape
    n_layers, n_phys, psz = cache_k.shape[:3]
    n_pages = page_table.shape[1]
    pages = PAGES_PER_STEP if n_pages % PAGES_PER_STEP == 0 else 1
    ck = cache_k.transpose(0, 1, 3, 4, 2).reshape(n_layers * n_phys, d, psz)
    cv = cache_v.transpose(0, 1, 3, 4, 2).reshape(n_layers * n_phys, d, psz)
    head_of_lane = jnp.arange(d) // hd
    qbd = jnp.where(head_of_lane[None, None, None, :] == jnp.arange(heads)[None, :, None, None],
                    q[:, None, :, :], jnp.zeros((), q.dtype)).reshape(db, heads * t_new, d)
    bias_rows = jnp.broadcast_to(jnp.repeat(bias.astype(F32), t_new)[:, None], (heads * t_new, psz))
    tri = jnp.concatenate([(jnp.arange(psz)[:, None] > jnp.arange(psz)[None, :]).astype(BF16),
                           jnp.ones((psz, psz), BF16)], axis=1)
    tri = jnp.tile(tri, (2, 1))
    base = layer * n_phys

    def page_spec(i):
        return pl.BlockSpec(
            (1, d, psz), lambda b, p, pt: (pt[b, n_pages - 1 - (p * pages + i)] + base, 0, 0))

    per_seq = lambda shape: pl.BlockSpec(shape, lambda b, p, pt: (b, 0, 0))
    const = lambda a: pl.BlockSpec(a.shape, lambda b, p, pt: (0,) * a.ndim)
    grid_spec = pltpu.PrefetchScalarGridSpec(
        num_scalar_prefetch=1,
        grid=(db, n_pages // pages),
        in_specs=[per_seq((1, heads * t_new, d)), const(bias_rows),
                  per_seq((1, t_new, d)), per_seq((1, t_new, d)), const(tri)]
                 + [page_spec(i) for i in range(pages)] * 2,
        out_specs=per_seq((1, t_new, d)),
        scratch_shapes=[pltpu.VMEM((heads * t_new, d), F32), pltpu.VMEM((heads * t_new, psz), F32),
                        pltpu.VMEM((psz, d), F32), pltpu.VMEM((psz, d), F32)],
    )
    return pl.pallas_call(
        functools.partial(_sb_sample_body, pages=pages, heads=heads, hd=hd),
        grid_spec=grid_spec,
        out_shape=jax.ShapeDtypeStruct((db, t_new, d), F32),
        compiler_params=_params("parallel", "arbitrary"),
    )(page_table, qbd, bias_rows, k_new, v_new, tri, *([ck] * pages), *([cv] * pages))


def kernel(x_prompt, x_sample, state_gla, cache_k, cache_v, page_table, p_prompt, p_sample,
           norm_ffn1, ffn1_w_in, ffn1_w_out, norm_mix, gla_w_in, gla_w_a2, gla_b_a, gla_norm,
           gla_w_out, sb_w_qkv, sb_bias, sb_w_out, norm_ffn2, ffn2_w_in, ffn2_w_out, norm_pe,
           pe_w_gate, pe_w_proj, norm_final):
    bp, tp, d = x_prompt.shape
    bs, ts, _ = x_sample.shape
    depth = norm_ffn1.shape[0]
    n_mixers = 2
    gla_heads, gla_dk, gla_dv = state_gla.shape[2:]
    dk_total, dv_total = gla_heads * gla_dk, gla_heads * gla_dv
    rank = gla_w_a2.shape[1]
    sb_heads, sb_hd = cache_k.shape[3:]
    row = lambda a: a.reshape(1, -1)

    groups = [dict(x=x_prompt.reshape(bp * tp, d), p=p_prompt.reshape(depth, bp * tp, -1), batch=bp),
              dict(x=x_sample.reshape(bs * ts, d), p=p_sample.reshape(depth, bs * ts, -1), batch=bs)]
    gla_states = [[], []]
    new_k = [[], []]
    new_v = [[], []]

    for i in range(depth):
        j = i // n_mixers
        w1_in, w1_out = ffn1_w_in[i].astype(BF16), ffn1_w_out[i].astype(BF16)
        w2_in, w2_out = ffn2_w_in[i].astype(BF16), ffn2_w_out[i].astype(BF16)
        wg, wp = pe_w_gate[i].astype(BF16), pe_w_proj[i].astype(BF16)
        if i % n_mixers == 0:
            w_main = gla_w_in[j, :, :2 * dk_total + 2 * dv_total].astype(BF16)
            w_a = jnp.pad(gla_w_in[j, :, 2 * dk_total + 2 * dv_total:],
                          ((0, 0), (0, LANES - rank))).astype(BF16)
            w_a2 = jnp.pad(gla_w_a2[j], ((0, LANES - rank), (0, 0))).astype(BF16)
            w_o = gla_w_out[j].astype(BF16)
        else:
            w_qkv = sb_w_qkv[j].astype(BF16)
            w_o = sb_w_out[j].astype(BF16)

        for gi, grp in enumerate(groups):
            x = grp["x"]
            is_prompt = gi == 0
            (x,) = _row_call(_ffn_body, [x], [row(norm_ffn1[i]), w1_in, w1_out], [(d, F32)], 512)
            if i % n_mixers == 0:
                q, k, v, r, g = _row_call(
                    _gla_proj_body, [x], [row(norm_mix[i]), w_main, w_a, w_a2, row(gla_b_a[j])],
                    [(dk_total, F32), (dk_total, F32), (dv_total, F32), (dv_total, F32), (dk_total, F32)],
                    512, dk_total=dk_total, dv_total=dv_total, q_scale=gla_dk ** -0.5)
                if is_prompt:
                    s0 = jnp.zeros((bp,) + state_gla.shape[2:], F32)
                    x, s_fin = _gla_mix(x, q, k, v, r, g, s0, row(gla_norm[j]), w_o,
                                        batch=bp, chunk=GLA_CHUNK, tile=min(512, tp))
                else:
                    x, s_fin = _gla_mix(x, q, k, v, r, g, state_gla[j].astype(F32), row(gla_norm[j]),
                                        w_o, batch=bs, chunk=ts, tile=ts)
                gla_states[gi].append(s_fin.astype(state_gla.dtype))
            else:
                if is_prompt:
                    q, kt, vt, ktb, vtb = _sb_proj_t(
                        x, row(norm_mix[i]), w_qkv[:, :d], w_qkv[:, d:2 * d].T, w_qkv[:, 2 * d:].T,
                        batch=bp, tm=512, blk=min(SB_BLOCK, tp), q_scale=sb_hd ** -0.5)
                    o = _sb_prompt(q, ktb, vtb, sb_bias[j].astype(F32), batch=bp, hd=sb_hd)
                    k, v = (a.reshape(bp, sb_heads, sb_hd, tp).transpose(0, 3, 1, 2) for a in (kt, vt))
                else:
                    q, k, v = _row_call(
                        _sb_proj_body, [x], [row(norm_mix[i]), w_qkv],
                        [(d, BF16), (d, F32), (d, F32)], 512, d=d, q_scale=sb_hd ** -0.5)
                    o = _sb_sample(q.reshape(bs, ts, d), k.reshape(bs, ts, d), v.reshape(bs, ts, d),
                                   sb_bias[j], cache_k, cache_v, j, page_table,
                                   heads=sb_heads, hd=sb_hd).reshape(bs * ts, d)
                    k, v = (a.reshape(bs, ts, sb_heads, sb_hd) for a in (k, v))
                (x,) = _row_call(_out_proj_body, [x, o], [w_o], [(d, F32)], 1024)
                new_k[gi].append(k.astype(cache_k.dtype))
                new_v[gi].append(v.astype(cache_v.dtype))
            (x,) = _row_call(_ffn_body, [x], [row(norm_ffn2[i]), w2_in, w2_out], [(d, F32)], 512)
            (x,) = _row_call(_ple_body, [x, grp["p"][i]],
                             [row(norm_pe[i]), wg, wp, row(norm_final)], [(d, F32)], 1024,
                             final=(i == depth - 1))
            grp["x"] = x

    return (groups[0]["x"].reshape(bp, tp, d), groups[1]["x"].reshape(bs, ts, d),
            jnp.stack(gla_states[0]), jnp.stack(gla_states[1]),
            jnp.stack(new_k[0]), jnp.stack(new_v[0]), jnp.stack(new_k[1]), jnp.stack(new_v[1]))
```

```python
import functools

import jax
import jax.numpy as jnp
from jax import lax
from jax.experimental import pallas as pl
from jax.experimental.pallas import tpu as pltpu

F32 = jnp.float32
BF16 = jnp.bfloat16

NORM_EPS = 1e-6
LOG2E = 1.4426950408889634
GLA_TAU = 16.0
GLA_CHUNK = 64
SB_BLOCK = 256
SB_UNROLL = 4
PAGES_PER_STEP = 8
GLA_MAX_CHUNK_DECAY = 80.0
LANES = 128
SUBLANES = 8
VMEM_LIMIT = 56 * 1024 * 1024


def _params(*sem):
    return pltpu.CompilerParams(dimension_semantics=sem, vmem_limit_bytes=VMEM_LIMIT)


def _rms(x, g):
    return x * lax.rsqrt(jnp.mean(x * x, axis=-1, keepdims=True) + NORM_EPS) * g


def _dot(a, b):
    return jnp.dot(a, b, preferred_element_type=F32)


def _dot_nt(a, b):
    return lax.dot_general(a, b, (((1,), (1,)), ((), ())), preferred_element_type=F32)


def _dot_tn(a, b):
    return lax.dot_general(a, b, (((0,), (0,)), ((), ())), preferred_element_type=F32)


def _softplus(z):
    neg_abs = lax.bitcast_convert_type(
        lax.bitcast_convert_type(z, jnp.uint32) | jnp.uint32(0x80000000), F32)
    return jnp.maximum(z, 0.0) + jnp.log(1.0 + jnp.exp(neg_abs))


def _split_bf16(x):
    hi = x.astype(BF16)
    lo = (x - hi.astype(F32)).astype(BF16)
    return hi, lo


def _const_spec(a):
    nd = a.ndim
    return pl.BlockSpec(a.shape, lambda *_: (0,) * nd, pipeline_mode=pl.Buffered(1))


def _row_call(body, row_args, const_args, out_dims, tm, **kw):
    n = row_args[0].shape[0]
    tm = min(tm, n)
    assert n % tm == 0
    in_specs = [pl.BlockSpec((tm, a.shape[1]), lambda i: (i, 0)) for a in row_args]
    in_specs += [_const_spec(a) for a in const_args]
    out_specs = [pl.BlockSpec((tm, d), lambda i: (i, 0)) for d, _ in out_dims]
    out_shape = [jax.ShapeDtypeStruct((n, d), dt) for d, dt in out_dims]
    outs = pl.pallas_call(
        functools.partial(body, **kw),
        grid=(n // tm,),
        in_specs=in_specs,
        out_specs=out_specs,
        out_shape=out_shape,
        compiler_params=_params("parallel"),
    )(*row_args, *const_args)
    return outs


def _ffn(x, g_ref, win_ref, wout_ref):
    xn = _rms(x, g_ref[...]).astype(BF16)
    h = _dot(xn, win_ref[...])
    dff = wout_ref.shape[0]
    gate, up = h[:, :dff], h[:, dff:]
    act = (gate * jax.nn.sigmoid(gate) * up).astype(BF16)
    return x + 0.5 * _dot(act, wout_ref[...])


def _ffn_body(x_ref, g_ref, win_ref, wout_ref, o_ref):
    o_ref[...] = _ffn(x_ref[...], g_ref, win_ref, wout_ref)


def _post_mixer_body(*refs, has_mix, final):
    x_ref, *refs = refs
    x = x_ref[...]
    if has_mix:
        mix_ref, *refs = refs
    p_ref, *refs = refs
    if has_mix:
        wmix_ref, *refs = refs
        x = x + _dot(mix_ref[...].astype(BF16), wmix_ref[...])
    g_ref, win_ref, wout_ref, gpe_ref, wg_ref, wp_ref, gf_ref, o_ref = refs
    x = _ffn(x, g_ref, win_ref, wout_ref)
    xn = _rms(x, gpe_ref[...]).astype(BF16)
    gate = jax.nn.sigmoid(_dot(xn, wg_ref[...]))
    x = x + gate * _dot(p_ref[...].astype(BF16), wp_ref[...])
    if final:
        x = _rms(x, gf_ref[...])
    o_ref[...] = x


def _gla_proj_body(x_ref, g_ref, w_ref, wa_ref, wa2_ref, ba_ref,
                   q_ref, k_ref, v_ref, r_ref, gate_ref, *, dk_total, dv_total, q_scale):
    xn = _rms(x_ref[...], g_ref[...]).astype(BF16)
    y = _dot(xn, w_ref[...])
    q_ref[...] = y[:, :dk_total] * q_scale
    k_ref[...] = y[:, dk_total:2 * dk_total]
    v_ref[...] = y[:, 2 * dk_total:2 * dk_total + dv_total]
    r_ref[...] = y[:, 2 * dk_total + dv_total:]
    a = _dot(xn, wa_ref[...]).astype(BF16)
    ga = _dot(a, wa2_ref[...]) + ba_ref[...]
    gate_ref[...] = (jnp.minimum(ga, 0.0) - jnp.log1p(jnp.exp(-jnp.abs(ga)))) * (1.0 / GLA_TAU)


def _sb_proj_body(x_ref, g_ref, w_ref, q_ref, k_ref, v_ref, *, d, q_scale):
    xn = _rms(x_ref[...], g_ref[...]).astype(BF16)
    y = _dot(xn, w_ref[...])
    q_ref[...] = (y[:, :d] * q_scale).astype(BF16)
    k_ref[...] = y[:, d:2 * d]
    v_ref[...] = y[:, 2 * d:]


def _sb_proj_t_body(x_ref, g_ref, wq_ref, wkt_ref, wvt_ref, q_ref, kt_ref, vt_ref, ktb_ref, vtb_ref,
                    *, q_scale):
    xn = _rms(x_ref[...], g_ref[...]).astype(BF16)
    q_ref[...] = (_dot(xn, wq_ref[...]) * q_scale).astype(BF16)
    blk = ktb_ref.shape[3]
    for w_ref, t_ref, tb_ref in ((wkt_ref, kt_ref, ktb_ref), (wvt_ref, vt_ref, vtb_ref)):
        yt = _dot_nt(w_ref[...], xn)
        t_ref[0] = yt
        for j in range(tb_ref.shape[1]):
            tb_ref[0, j] = yt[:, j * blk:(j + 1) * blk].astype(BF16)


def _sb_proj_t(x, g, wq, wkt, wvt, *, batch, tm, blk, q_scale):
    n, d = x.shape
    seq = n // batch
    tm = min(tm, seq)
    nt = seq // tm
    row = pl.BlockSpec((tm, d), lambda b, t: (b * nt + t, 0))
    tr = pl.BlockSpec((1, d, tm), lambda b, t: (b, 0, t))
    trb = pl.BlockSpec((1, tm // blk, d, blk), lambda b, t: (b, t, 0, 0))
    return pl.pallas_call(
        functools.partial(_sb_proj_t_body, q_scale=q_scale),
        grid=(batch, nt),
        in_specs=[row, _const_spec(g), _const_spec(wq), _const_spec(wkt), _const_spec(wvt)],
        out_specs=[row, tr, tr, trb, trb],
        out_shape=[jax.ShapeDtypeStruct((n, d), BF16),
                   jax.ShapeDtypeStruct((batch, d, seq), F32),
                   jax.ShapeDtypeStruct((batch, d, seq), F32),
                   jax.ShapeDtypeStruct((batch, seq // blk, d, blk), BF16),
                   jax.ShapeDtypeStruct((batch, seq // blk, d, blk), BF16)],
        compiler_params=_params("parallel", "parallel"),
    )(x, g, wq, wkt, wvt)


def _gla_mix_body(x_ref, q_ref, k_ref, v_ref, r_ref, g_ref, s0_ref, gn_ref, wo_ref, tri_ref,
                  y_ref, sfin_ref, st_ref, o_ref, b_ref, qt_ref, kt_ref, ks_ref, dec_ref,
                  *, heads, chunk):
    t = pl.program_id(1)
    tile = x_ref.shape[0]
    dk = q_ref.shape[1] // heads
    dv = v_ref.shape[1] // heads

    @pl.when(t == 0)
    def _():
        for h in range(heads):
            st_ref[h] = s0_ref[0, h].T

    tri = tri_ref[...]
    causal = (lax.broadcasted_iota(jnp.int32, (chunk, chunk), 1)
              <= lax.broadcasted_iota(jnp.int32, (chunk, chunk), 0))

    n_chunks = tile // chunk
    kd = heads * dk

    for c in range(n_chunks):
        rows = slice(c * chunk, (c + 1) * chunk)
        g_hi, g_lo = _split_bf16(g_ref[rows, :])
        b_ref[rows, :] = _dot(tri, g_hi) + _dot(tri, g_lo)
    b3 = b_ref[...].reshape(n_chunks, chunk, kd)
    b_last = b3[:, chunk - 1:chunk, :]
    q3 = q_ref[...].reshape(n_chunks, chunk, kd)
    k3 = k_ref[...].reshape(n_chunks, chunk, kd)
    qt_ref[...] = (q3 * jnp.exp(b3)).astype(BF16).reshape(tile, kd)
    ks_ref[...] = (k3 * jnp.exp(b_last - b3)).astype(BF16).reshape(tile, kd)
    dec_ref[...] = jnp.broadcast_to(jnp.exp(b_last), (n_chunks, SUBLANES, kd))
    factorable = jnp.min(b_last) >= -GLA_MAX_CHUNK_DECAY

    def chunk_step(c, carry, *, factorable):
        rows = pl.ds(pl.multiple_of(c * chunk, chunk), chunk)
        q_t = qt_ref[rows, :]
        k_s = ks_ref[rows, :]
        v = v_ref[rows, :].astype(BF16)
        s_t = [st_ref[h] for h in range(heads)]
        hk = [slice(h * dk, (h + 1) * dk) for h in range(heads)]
        hv = [slice(h * dv, (h + 1) * dv) for h in range(heads)]
        o = [_dot_nt(q_t[:, hk[h]], s_t[h].astype(BF16)) for h in range(heads)]
        if factorable:
            k_t = kt_ref[rows, :]
            sc = [jnp.where(causal, _dot_nt(q_t[:, hk[h]], k_t[:, hk[h]]), 0.0).astype(BF16)
                  for h in range(heads)]
        upd = [_dot_tn(v[:, hv[h]], k_s[:, hk[h]]) for h in range(heads)]
        if factorable:
            o = [o[h] + _dot(sc[h], v[:, hv[h]]) for h in range(heads)]
        dec = dec_ref[c]
        for h in range(heads):
            o_ref[rows, hv[h]] = o[h]
            st_ref[h] = s_t[h] * dec[0:1, hk[h]] + upd[h]
        if factorable:
            return carry

        b = b_ref[rows, :]
        k = k_ref[rows, :]
        s_idx = lax.broadcasted_iota(jnp.int32, (chunk, 1), 0)
        sub = lax.broadcasted_iota(jnp.int32, (SUBLANES, 1), 0)

        def token_group(t8, carry):
            rows8 = pl.ds(pl.multiple_of(c * chunk + t8 * SUBLANES, SUBLANES), SUBLANES)
            q8 = q_ref[rows8, :]
            b8 = b_ref[rows8, :]
            out = [jnp.zeros((SUBLANES, dv), F32) for _ in range(heads)]
            for r in range(SUBLANES):
                t = t8 * SUBLANES + r
                decay = jnp.exp(jnp.where(s_idx <= t, b8[r:r + 1, :] - b, -jnp.inf))
                w = q8[r:r + 1, :] * decay * k
                for h in range(heads):
                    sc = jnp.sum(w[:, h * dk:(h + 1) * dk], axis=1, keepdims=True)
                    o_t = jnp.sum(sc * v_ref[rows, h * dv:(h + 1) * dv], axis=0, keepdims=True)
                    out[h] = jnp.where(sub == r, o_t, out[h])
            for h in range(heads):
                o_ref[rows8, h * dv:(h + 1) * dv] += out[h]
            return carry

        return lax.fori_loop(0, chunk // SUBLANES, token_group, carry)

    for flag, pred in ((True, factorable), (False, jnp.logical_not(factorable))):
        @pl.when(pred)
        def _():
            if flag:
                kt_ref[...] = (k3 * jnp.exp(-b3)).astype(BF16).reshape(tile, kd)
            lax.fori_loop(0, n_chunks, functools.partial(chunk_step, factorable=flag), 0)

    gn = gn_ref[...]
    r = r_ref[...]
    parts = []
    for h in range(heads):
        vs = slice(h * dv, (h + 1) * dv)
        parts.append(_rms(o_ref[:, vs], gn[:, vs]))
    on = jnp.concatenate(parts, axis=1)
    on = (on * (r * jax.nn.sigmoid(r))).astype(BF16)
    y_ref[...] = x_ref[...] + _dot(on, wo_ref[...])

    @pl.when(t == pl.num_programs(1) - 1)
    def _():
        for h in range(heads):
            sfin_ref[0, h] = st_ref[h].T


def _gla_mix(x, q, k, v, r, g, s0, gn, wo, *, batch, chunk, tile):
    n, d = x.shape
    seq = n // batch
    heads, dk, dv = s0.shape[1:]
    nt = seq // tile
    tri = jnp.tril(jnp.ones((chunk, chunk), F32)).astype(BF16)
    row = lambda w: pl.BlockSpec((tile, w), lambda b, t: (b * nt + t, 0))
    y, sfin = pl.pallas_call(
        functools.partial(_gla_mix_body, heads=heads, chunk=chunk),
        grid=(batch, nt),
        in_specs=[row(d), row(heads * dk), row(heads * dk), row(heads * dv), row(heads * dv),
                  row(heads * dk),
                  pl.BlockSpec((1, heads, dk, dv), lambda b, t: (b, 0, 0, 0)),
                  _const_spec(gn), _const_spec(wo), _const_spec(tri)],
        out_specs=[row(d), pl.BlockSpec((1, heads, dk, dv), lambda b, t: (b, 0, 0, 0))],
        out_shape=[jax.ShapeDtypeStruct((n, d), F32),
                   jax.ShapeDtypeStruct((batch, heads, dk, dv), F32)],
        scratch_shapes=[pltpu.VMEM((heads, dv, dk), F32), pltpu.VMEM((tile, heads * dv), F32),
                        pltpu.VMEM((tile, heads * dk), F32)]
                       + [pltpu.VMEM((tile, heads * dk), BF16)] * 3
                       + [pltpu.VMEM((tile // chunk, SUBLANES, heads * dk), F32)],
        compiler_params=_params("arbitrary", "arbitrary"),
    )(x, q, k, v, r, g, s0, gn, wo, tri)
    return y, sfin


def _sb_prompt_body(bias_ref, q_ref, k_ref, v_ref, tri_ref, o_ref,
                    z_ref, lb_ref, hi_ref, la_ref, zb_ref, qh_ref, acc_ref, run_ref, *, blk, hd):
    h2 = pl.program_id(1)
    nq = k_ref.shape[1]
    n_pairs = nq * (nq + 1) // 2
    tri = tri_ref[...]
    lane = lax.broadcasted_iota(jnp.int32, (blk, LANES), 1)
    row = lax.broadcasted_iota(jnp.int32, (blk, blk), 0)
    col = lax.broadcasted_iota(jnp.int32, (blk, blk), 1)
    heads = range(LANES // hd)
    in_head = [(lane >= i * hd) & (lane < (i + 1) * hd) for i in heads]

    q_all = q_ref[0]
    q_lane = lax.broadcasted_iota(jnp.int32, q_all.shape, 1)
    for i in heads:
        b = bias_ref[h2 * len(heads) + i]
        zb_ref[i, 0] = jnp.full((blk, blk), b, F32)
        zb_ref[i, 1] = jnp.where(col < row, b, -jnp.inf)
        qh_ref[i] = jnp.where((q_lane >= i * hd) & (q_lane < (i + 1) * hd), q_all,
                              jnp.zeros_like(q_all))
    z_ref[...] = jnp.zeros_like(z_ref)
    lb_ref[...] = jnp.zeros_like(lb_ref)
    hi_ref[...] = jnp.zeros_like(hi_ref)
    la_ref[...] = jnp.zeros_like(la_ref)
    acc_ref[...] = jnp.zeros_like(acc_ref)
    run_ref[...] = jnp.zeros_like(run_ref)

    def following(qi, kb):
        step_q = jnp.minimum(qi + 1, nq - 1)
        return jnp.where(kb > 0, qi, step_q), jnp.where(kb > 0, kb - 1, step_q)

    def body(pairs, slot):
        (qf, kf), (qs, ks), (qm, km), (qb, kb) = pairs
        vt = v_ref[0, kb]
        for i in heads:
            a = jnp.exp(la_ref[i]).astype(BF16)
            acc_ref[i] = jnp.where(kb == qb, 0.0, acc_ref[i]) + _dot_nt(a, vt)
        q_rows = pl.ds(pl.multiple_of(qf * blk, blk), blk)
        kt = k_ref[0, kf]
        diag = (kf == qf).astype(jnp.int32)
        for i in heads:
            z_ref[i, 1 - slot] = _dot(qh_ref[i, q_rows, :], kt) + zb_ref[i, diag]
        for i in heads:
            hi = hi_ref[i]
            later = _dot(hi, tri)
            run = jnp.where(km == qm, 0.0, run_ref[i])
            la_ref[i] = lb_ref[i] - later + run
            run_ref[i] = run - (later[:, 0:1] + hi[:, 0:1].astype(F32))
        for i in heads:
            zi = z_ref[i, slot]
            s = _softplus(zi)
            lb_ref[i] = zi - s
            hi_ref[i] = s.astype(BF16)
        o = acc_ref[0]
        for i in heads[1:]:
            o = jnp.where(in_head[i], acc_ref[i], o)
        o_ref[0, pl.ds(pl.multiple_of(qb * blk, blk), blk), :] = o.astype(o_ref.dtype)
        return following(qf, kf), (qf, kf), (qs, ks), (qm, km)

    def unrolled(_, pairs):
        for u in range(SB_UNROLL):
            pairs = body(pairs, u % 2)
        return pairs

    zero = jnp.int32(0)
    steps = n_pairs + 3
    pairs = lax.fori_loop(0, steps // SB_UNROLL, unrolled, ((zero, zero),) * 4)
    for u in range(steps % SB_UNROLL):
        pairs = body(pairs, u % 2)


def _sb_prompt(q, kt, vt, bias, *, batch, hd):
    n, d = q.shape
    seq = n // batch
    nq, blk = kt.shape[1], kt.shape[3]
    q3 = q.reshape(batch, seq, d)
    tri = (jnp.arange(blk)[:, None] > jnp.arange(blk)[None, :]).astype(BF16)
    heads = LANES // hd
    q_spec = pl.BlockSpec((1, seq, LANES), lambda b, h: (b, 0, h))
    kv_spec = pl.BlockSpec((1, nq, LANES, blk), lambda b, h: (b, 0, h, 0))
    stage = lambda dt: pltpu.VMEM((heads, blk, blk), dt)
    o = pl.pallas_call(
        functools.partial(_sb_prompt_body, blk=blk, hd=hd),
        grid=(batch, d // LANES),
        in_specs=[pl.BlockSpec(memory_space=pltpu.SMEM), q_spec, kv_spec, kv_spec, _const_spec(tri)],
        out_specs=q_spec,
        out_shape=jax.ShapeDtypeStruct((batch, seq, d), BF16),
        scratch_shapes=[pltpu.VMEM((heads, 2, blk, blk), F32), stage(F32), stage(BF16), stage(F32),
                        pltpu.VMEM((heads, 2, blk, blk), F32),
                        pltpu.VMEM((heads, seq, LANES), BF16),
                        pltpu.VMEM((heads, blk, LANES), F32), pltpu.VMEM((heads, blk, 1), F32)],
        compiler_params=_params("parallel", "parallel"),
    )(bias, q3, kt, vt, tri)
    return o.reshape(n, d)


def _sb_sample_body(pt_ref, qbd_ref, bias_ref, kn_ref, vn_ref, tri_ref, *rest, pages, heads, hd):
    k_refs = rest[:pages]
    v_refs = rest[pages:2 * pages]
    o_ref, acc_ref, run_ref, kpad_ref, vpad_ref = rest[2 * pages:]
    p = pl.program_id(1)
    t_new = kn_ref.shape[1]
    rows = qbd_ref.shape[1]
    psz = kpad_ref.shape[0]
    qbd = qbd_ref[0]
    bias = bias_ref[...]
    tri = tri_ref[...]

    def sweep(kps, vps, causal):
        z = [(_dot(qbd, kp) if causal is None else _dot_nt(qbd, kp)) + bias for kp in kps]
        sp = [_softplus(zi) for zi in z]
        lb = [zi - si for zi, si in zip(z, sp)]
        if causal is not None:
            sp = [jnp.where(causal, si, 0.0) for si in sp]
            lb = [jnp.where(causal, li, -jnp.inf) for li in lb]
        both = [_dot(jnp.concatenate(_split_bf16(si), axis=1), tri) for si in sp]
        run = run_ref[...]
        acc = acc_ref[...]
        for li, bi, vp in zip(lb, both, vps):
            a = jnp.exp(li - bi[:, :psz] + run).astype(BF16)
            run = run - bi[:, psz:]
            acc = acc + (_dot_nt(a, vp) if causal is None else _dot(a, vp))
        run_ref[...] = run
        acc_ref[...] = acc

    @pl.when(p == 0)
    def _():
        acc_ref[...] = jnp.zeros_like(acc_ref)
        run_ref[...] = jnp.zeros_like(run_ref)
        kpad_ref[...] = jnp.zeros_like(kpad_ref)
        vpad_ref[...] = jnp.zeros_like(vpad_ref)
        kpad_ref[0:t_new, :] = kn_ref[0]
        vpad_ref[0:t_new, :] = vn_ref[0]
        tq = lax.broadcasted_iota(jnp.int32, (rows, psz), 0) % t_new
        causal = lax.broadcasted_iota(jnp.int32, (rows, psz), 1) < tq
        sweep([kpad_ref[...].astype(BF16)], [vpad_ref[...].astype(BF16)], causal)

    sweep([r[0].astype(BF16) for r in k_refs], [r[0].astype(BF16) for r in v_refs], None)

    @pl.when(p == pl.num_programs(1) - 1)
    def _():
        lane = lax.broadcasted_iota(jnp.int32, (t_new, LANES), 1)
        per_tile = LANES // hd
        for gidx in range(heads // per_tile):
            cols = slice(gidx * LANES, (gidx + 1) * LANES)
            o = acc_ref[gidx * per_tile * t_new:(gidx * per_tile + 1) * t_new, cols]
            for i in range(1, per_tile):
                h = gidx * per_tile + i
                o = jnp.where(lane >= i * hd, acc_ref[h * t_new:(h + 1) * t_new, cols], o)
            o_ref[0, :, cols] = o


def _sb_sample(q, k_new, v_new, bias, cache_k, cache_v, layer, page_table, *, heads, hd):
    db, t_new, d = q.shape
    n_layers, n_phys, psz = cache_k.shape[:3]
    n_pages = page_table.shape[1]
    pages = PAGES_PER_STEP if n_pages % PAGES_PER_STEP == 0 else 1
    ck = cache_k.transpose(0, 1, 3, 4, 2).reshape(n_layers * n_phys, d, psz)
    cv = cache_v.transpose(0, 1, 3, 4, 2).reshape(n_layers * n_phys, d, psz)
    head_of_lane = jnp.arange(d) // hd
    qbd = jnp.where(head_of_lane[None, None, None, :] == jnp.arange(heads)[None, :, None, None],
                    q[:, None, :, :], jnp.zeros((), q.dtype)).reshape(db, heads * t_new, d)
    bias_rows = jnp.broadcast_to(jnp.repeat(bias.astype(F32), t_new)[:, None], (heads * t_new, psz))
    tri = jnp.concatenate([(jnp.arange(psz)[:, None] > jnp.arange(psz)[None, :]).astype(BF16),
                           jnp.ones((psz, psz), BF16)], axis=1)
    tri = jnp.tile(tri, (2, 1))
    base = layer * n_phys

    def page_spec(i):
        return pl.BlockSpec(
            (1, d, psz), lambda b, p, pt: (pt[b, n_pages - 1 - (p * pages + i)] + base, 0, 0))

    per_seq = lambda shape: pl.BlockSpec(shape, lambda b, p, pt: (b, 0, 0))
    const = lambda a: pl.BlockSpec(a.shape, lambda b, p, pt: (0,) * a.ndim)
    grid_spec = pltpu.PrefetchScalarGridSpec(
        num_scalar_prefetch=1,
        grid=(db, n_pages // pages),
        in_specs=[per_seq((1, heads * t_new, d)), const(bias_rows),
                  per_seq((1, t_new, d)), per_seq((1, t_new, d)), const(tri)]
                 + [page_spec(i) for i in range(pages)] * 2,
        out_specs=per_seq((1, t_new, d)),
        scratch_shapes=[pltpu.VMEM((heads * t_new, d), F32), pltpu.VMEM((heads * t_new, psz), F32),
                        pltpu.VMEM((psz, d), F32), pltpu.VMEM((psz, d), F32)],
    )
    return pl.pallas_call(
        functools.partial(_sb_sample_body, pages=pages, heads=heads, hd=hd),
        grid_spec=grid_spec,
        out_shape=jax.ShapeDtypeStruct((db, t_new, d), F32),
        compiler_params=_params("parallel", "arbitrary"),
    )(page_table, qbd, bias_rows, k_new, v_new, tri, *([ck] * pages), *([cv] * pages))


def kernel(x_prompt, x_sample, state_gla, cache_k, cache_v, page_table, p_prompt, p_sample,
           norm_ffn1, ffn1_w_in, ffn1_w_out, norm_mix, gla_w_in, gla_w_a2, gla_b_a, gla_norm,
           gla_w_out, sb_w_qkv, sb_bias, sb_w_out, norm_ffn2, ffn2_w_in, ffn2_w_out, norm_pe,
           pe_w_gate, pe_w_proj, norm_final):
    bp, tp, d = x_prompt.shape
    bs, ts, _ = x_sample.shape
    depth = norm_ffn1.shape[0]
    n_mixers = 2
    gla_heads, gla_dk, gla_dv = state_gla.shape[2:]
    dk_total, dv_total = gla_heads * gla_dk, gla_heads * gla_dv
    rank = gla_w_a2.shape[1]
    sb_heads, sb_hd = cache_k.shape[3:]
    row = lambda a: a.reshape(1, -1)

    groups = [dict(x=x_prompt.reshape(bp * tp, d), p=p_prompt.reshape(depth, bp * tp, -1), batch=bp),
              dict(x=x_sample.reshape(bs * ts, d), p=p_sample.reshape(depth, bs * ts, -1), batch=bs)]
    gla_states = [[], []]
    new_k = [[], []]
    new_v = [[], []]

    for i in range(depth):
        j = i // n_mixers
        w1_in, w1_out = ffn1_w_in[i].astype(BF16), ffn1_w_out[i].astype(BF16)
        w2_in, w2_out = ffn2_w_in[i].astype(BF16), ffn2_w_out[i].astype(BF16)
        wg, wp = pe_w_gate[i].astype(BF16), pe_w_proj[i].astype(BF16)
        if i % n_mixers == 0:
            w_main = gla_w_in[j, :, :2 * dk_total + 2 * dv_total].astype(BF16)
            w_a = jnp.pad(gla_w_in[j, :, 2 * dk_total + 2 * dv_total:],
                          ((0, 0), (0, LANES - rank))).astype(BF16)
            w_a2 = jnp.pad(gla_w_a2[j], ((0, LANES - rank), (0, 0))).astype(BF16)
            w_o = gla_w_out[j].astype(BF16)
        else:
            w_qkv = sb_w_qkv[j].astype(BF16)
            w_o = sb_w_out[j].astype(BF16)

        for gi, grp in enumerate(groups):
            x = grp["x"]
            is_prompt = gi == 0
            (x,) = _row_call(_ffn_body, [x], [row(norm_ffn1[i]), w1_in, w1_out], [(d, F32)], 512)
            if i % n_mixers == 0:
                q, k, v, r, g = _row_call(
                    _gla_proj_body, [x], [row(norm_mix[i]), w_main, w_a, w_a2, row(gla_b_a[j])],
                    [(dk_total, F32), (dk_total, F32), (dv_total, F32), (dv_total, F32), (dk_total, F32)],
                    512, dk_total=dk_total, dv_total=dv_total, q_scale=gla_dk ** -0.5)
                if is_prompt:
                    s0 = jnp.zeros((bp,) + state_gla.shape[2:], F32)
                    x, s_fin = _gla_mix(x, q, k, v, r, g, s0, row(gla_norm[j]), w_o,
                                        batch=bp, chunk=GLA_CHUNK, tile=min(512, tp))
                else:
                    x, s_fin = _gla_mix(x, q, k, v, r, g, state_gla[j].astype(F32), row(gla_norm[j]),
                                        w_o, batch=bs, chunk=ts, tile=ts)
                gla_states[gi].append(s_fin.astype(state_gla.dtype))
            else:
                if is_prompt:
                    q, kt, vt, ktb, vtb = _sb_proj_t(
                        x, row(norm_mix[i]), w_qkv[:, :d], w_qkv[:, d:2 * d].T, w_qkv[:, 2 * d:].T,
                        batch=bp, tm=512, blk=min(SB_BLOCK, tp), q_scale=sb_hd ** -0.5)
                    o = _sb_prompt(q, ktb, vtb, sb_bias[j].astype(F32), batch=bp, hd=sb_hd)
                    k, v = (a.reshape(bp, sb_heads, sb_hd, tp).transpose(0, 3, 1, 2) for a in (kt, vt))
                else:
                    q, k, v = _row_call(
                        _sb_proj_body, [x], [row(norm_mix[i]), w_qkv],
                        [(d, BF16), (d, F32), (d, F32)], 512, d=d, q_scale=sb_hd ** -0.5)
                    o = _sb_sample(q.reshape(bs, ts, d), k.reshape(bs, ts, d), v.reshape(bs, ts, d),
                                   sb_bias[j], cache_k, cache_v, j, page_table,
                                   heads=sb_heads, hd=sb_hd).reshape(bs * ts, d)
                    k, v = (a.reshape(bs, ts, sb_heads, sb_hd) for a in (k, v))
                new_k[gi].append(k.astype(cache_k.dtype))
                new_v[gi].append(v.astype(cache_v.dtype))
            has_mix = i % n_mixers != 0
            (x,) = _row_call(
                _post_mixer_body, [x] + ([o] if has_mix else []) + [grp["p"][i]],
                ([w_o] if has_mix else []) + [row(norm_ffn2[i]), w2_in, w2_out, row(norm_pe[i]), wg, wp,
                                              row(norm_final)],
                [(d, F32)], 512, has_mix=has_mix, final=(i == depth - 1))
            grp["x"] = x

    return (groups[0]["x"].reshape(bp, tp, d), groups[1]["x"].reshape(bs, ts, d),
            jnp.stack(gla_states[0]), jnp.stack(gla_states[1]),
            jnp.stack(new_k[0]), jnp.stack(new_v[0]), jnp.stack(new_k[1]), jnp.stack(new_v[1]))
```

```python
import functools

import jax
import jax.numpy as jnp
from jax import lax
from jax.experimental import pallas as pl
from jax.experimental.pallas import tpu as pltpu

F32 = jnp.float32
BF16 = jnp.bfloat16

NORM_EPS = 1e-6
LOG2E = 1.4426950408889634
GLA_TAU = 16.0
GLA_CHUNK = 64
SB_BLOCK = 256
SB_UNROLL = 8
PAGES_PER_STEP = 8
GLA_MAX_CHUNK_DECAY = 80.0
LANES = 128
SUBLANES = 8
VMEM_LIMIT = 56 * 1024 * 1024


def _params(*sem):
    return pltpu.CompilerParams(dimension_semantics=sem, vmem_limit_bytes=VMEM_LIMIT)


def _rms(x, g):
    return x * lax.rsqrt(jnp.mean(x * x, axis=-1, keepdims=True) + NORM_EPS) * g


def _dot(a, b):
    return jnp.dot(a, b, preferred_element_type=F32)


def _dot_nt(a, b):
    return lax.dot_general(a, b, (((1,), (1,)), ((), ())), preferred_element_type=F32)


def _dot_tn(a, b):
    return lax.dot_general(a, b, (((0,), (0,)), ((), ())), preferred_element_type=F32)


def _softplus(z):
    neg_abs = lax.bitcast_convert_type(
        lax.bitcast_convert_type(z, jnp.uint32) | jnp.uint32(0x80000000), F32)
    return jnp.maximum(z, 0.0) + jnp.log(1.0 + jnp.exp(neg_abs))


def _split_bf16(x):
    hi = x.astype(BF16)
    lo = (x - hi.astype(F32)).astype(BF16)
    return hi, lo


def _const_spec(a):
    nd = a.ndim
    return pl.BlockSpec(a.shape, lambda *_: (0,) * nd, pipeline_mode=pl.Buffered(1))


def _row_call(body, row_args, const_args, out_dims, tm, **kw):
    n = row_args[0].shape[0]
    tm = min(tm, n)
    assert n % tm == 0
    row_args = [a if isinstance(a, tuple) else (a, 0) for a in row_args]
    assert all(first % tm == 0 for _, first in row_args)
    in_specs = [pl.BlockSpec((tm, a.shape[1]), lambda i, off=first // tm: (i + off, 0))
                for a, first in row_args]
    row_args = [a for a, _ in row_args]
    in_specs += [_const_spec(a) for a in const_args]
    out_specs = [pl.BlockSpec((tm, d), lambda i: (i, 0)) for d, _ in out_dims]
    out_shape = [jax.ShapeDtypeStruct((n, d), dt) for d, dt in out_dims]
    outs = pl.pallas_call(
        functools.partial(body, **kw),
        grid=(n // tm,),
        in_specs=in_specs,
        out_specs=out_specs,
        out_shape=out_shape,
        compiler_params=_params("parallel"),
    )(*row_args, *const_args)
    return outs


def _ffn(x, g_ref, win_ref, wout_ref):
    xn = _rms(x, g_ref[...]).astype(BF16)
    h = _dot(xn, win_ref[...])
    dff = wout_ref.shape[0]
    gate, up = h[:, :dff], h[:, dff:]
    act = (gate * jax.nn.sigmoid(gate) * up).astype(BF16)
    return x + 0.5 * _dot(act, wout_ref[...])


def _ffn_body(x_ref, g_ref, win_ref, wout_ref, o_ref):
    o_ref[...] = _ffn(x_ref[...], g_ref, win_ref, wout_ref)


def _post_mixer_body(*refs, has_mix, final):
    x_ref, *refs = refs
    x = x_ref[...]
    if has_mix:
        mix_ref, *refs = refs
    p_ref, *refs = refs
    if has_mix:
        wmix_ref, *refs = refs
        x = x + _dot(mix_ref[...].astype(BF16), wmix_ref[...])
    g_ref, win_ref, wout_ref, gpe_ref, wg_ref, wp_ref, gf_ref, o_ref = refs
    x = _ffn(x, g_ref, win_ref, wout_ref)
    xn = _rms(x, gpe_ref[...]).astype(BF16)
    gate = jax.nn.sigmoid(_dot(xn, wg_ref[...]))
    x = x + gate * _dot(p_ref[...].astype(BF16), wp_ref[...])
    if final:
        x = _rms(x, gf_ref[...])
    o_ref[...] = x


def _gla_proj_body(x_ref, g_ref, w_ref, wa_ref, wa2_ref, ba_ref,
                   q_ref, k_ref, v_ref, r_ref, gate_ref, *, dk_total, dv_total, q_scale):
    xn = _rms(x_ref[...], g_ref[...]).astype(BF16)
    y = _dot(xn, w_ref[...])
    q_ref[...] = y[:, :dk_total] * q_scale
    k_ref[...] = y[:, dk_total:2 * dk_total]
    v_ref[...] = y[:, 2 * dk_total:2 * dk_total + dv_total]
    r_ref[...] = y[:, 2 * dk_total + dv_total:]
    a = _dot(xn, wa_ref[...]).astype(BF16)
    ga = _dot(a, wa2_ref[...]) + ba_ref[...]
    gate_ref[...] = (jnp.minimum(ga, 0.0) - jnp.log1p(jnp.exp(-jnp.abs(ga)))) * (1.0 / GLA_TAU)


def _sb_proj_body(x_ref, g_ref, w_ref, q_ref, k_ref, v_ref, *, d, q_scale):
    xn = _rms(x_ref[...], g_ref[...]).astype(BF16)
    y = _dot(xn, w_ref[...])
    q_ref[...] = (y[:, :d] * q_scale).astype(BF16)
    k_ref[...] = y[:, d:2 * d]
    v_ref[...] = y[:, 2 * d:]


def _sb_proj_t_body(x_ref, g_ref, wq_ref, wkt_ref, wvt_ref, q_ref, kt_ref, vt_ref, ktb_ref, vtb_ref,
                    *, q_scale):
    xn = _rms(x_ref[...], g_ref[...]).astype(BF16)
    q_ref[...] = (_dot(xn, wq_ref[...]) * q_scale).astype(BF16)
    blk = ktb_ref.shape[3]
    for w_ref, t_ref, tb_ref in ((wkt_ref, kt_ref, ktb_ref), (wvt_ref, vt_ref, vtb_ref)):
        yt = _dot_nt(w_ref[...], xn)
        t_ref[0] = yt
        for j in range(tb_ref.shape[1]):
            tb_ref[0, j] = yt[:, j * blk:(j + 1) * blk].astype(BF16)


def _sb_proj_t(x, g, wq, wkt, wvt, *, batch, tm, blk, q_scale):
    n, d = x.shape
    seq = n // batch
    tm = min(tm, seq)
    nt = seq // tm
    row = pl.BlockSpec((tm, d), lambda b, t: (b * nt + t, 0))
    tr = pl.BlockSpec((1, d, tm), lambda b, t: (b, 0, t))
    trb = pl.BlockSpec((1, tm // blk, d, blk), lambda b, t: (b, t, 0, 0))
    return pl.pallas_call(
        functools.partial(_sb_proj_t_body, q_scale=q_scale),
        grid=(batch, nt),
        in_specs=[row, _const_spec(g), _const_spec(wq), _const_spec(wkt), _const_spec(wvt)],
        out_specs=[row, tr, tr, trb, trb],
        out_shape=[jax.ShapeDtypeStruct((n, d), BF16),
                   jax.ShapeDtypeStruct((batch, d, seq), F32),
                   jax.ShapeDtypeStruct((batch, d, seq), F32),
                   jax.ShapeDtypeStruct((batch, seq // blk, d, blk), BF16),
                   jax.ShapeDtypeStruct((batch, seq // blk, d, blk), BF16)],
        compiler_params=_params("parallel", "parallel"),
    )(x, g, wq, wkt, wvt)


def _gla_mix_body(x_ref, q_ref, k_ref, v_ref, r_ref, g_ref, s0_ref, gn_ref, wo_ref, tri_ref,
                  y_ref, sfin_ref, st_ref, o_ref, b_ref, qt_ref, kt_ref, ks_ref, dec_ref,
                  *, heads, chunk):
    t = pl.program_id(1)
    tile = x_ref.shape[0]
    dk = q_ref.shape[1] // heads
    dv = v_ref.shape[1] // heads

    @pl.when(t == 0)
    def _():
        for h in range(heads):
            st_ref[h] = s0_ref[0, h].T

    tri = tri_ref[...]
    causal = (lax.broadcasted_iota(jnp.int32, (chunk, chunk), 1)
              <= lax.broadcasted_iota(jnp.int32, (chunk, chunk), 0))

    n_chunks = tile // chunk
    kd = heads * dk

    for c in range(n_chunks):
        rows = slice(c * chunk, (c + 1) * chunk)
        g_hi, g_lo = _split_bf16(g_ref[rows, :])
        b_ref[rows, :] = _dot(tri, g_hi) + _dot(tri, g_lo)
    b3 = b_ref[...].reshape(n_chunks, chunk, kd)
    b_last = b3[:, chunk - 1:chunk, :]
    q3 = q_ref[...].reshape(n_chunks, chunk, kd)
    k3 = k_ref[...].reshape(n_chunks, chunk, kd)
    qt_ref[...] = (q3 * jnp.exp(b3)).astype(BF16).reshape(tile, kd)
    ks_ref[...] = (k3 * jnp.exp(b_last - b3)).astype(BF16).reshape(tile, kd)
    dec_ref[...] = jnp.broadcast_to(jnp.exp(b_last), (n_chunks, SUBLANES, kd))
    factorable = jnp.min(b_last) >= -GLA_MAX_CHUNK_DECAY

    def chunk_step(c, carry, *, factorable):
        rows = pl.ds(pl.multiple_of(c * chunk, chunk), chunk)
        q_t = qt_ref[rows, :]
        k_s = ks_ref[rows, :]
        v = v_ref[rows, :].astype(BF16)
        s_t = [st_ref[h] for h in range(heads)]
        hk = [slice(h * dk, (h + 1) * dk) for h in range(heads)]
        hv = [slice(h * dv, (h + 1) * dv) for h in range(heads)]
        o = [_dot_nt(q_t[:, hk[h]], s_t[h].astype(BF16)) for h in range(heads)]
        if factorable:
            k_t = kt_ref[rows, :]
            sc = [jnp.where(causal, _dot_nt(q_t[:, hk[h]], k_t[:, hk[h]]), 0.0).astype(BF16)
                  for h in range(heads)]
        upd = [_dot_tn(v[:, hv[h]], k_s[:, hk[h]]) for h in range(heads)]
        if factorable:
            o = [o[h] + _dot(sc[h], v[:, hv[h]]) for h in range(heads)]
        dec = dec_ref[c]
        for h in range(heads):
            o_ref[rows, hv[h]] = o[h]
            st_ref[h] = s_t[h] * dec[0:1, hk[h]] + upd[h]
        if factorable:
            return carry

        b = b_ref[rows, :]
        k = k_ref[rows, :]
        s_idx = lax.broadcasted_iota(jnp.int32, (chunk, 1), 0)
        sub = lax.broadcasted_iota(jnp.int32, (SUBLANES, 1), 0)

        def token_group(t8, carry):
            rows8 = pl.ds(pl.multiple_of(c * chunk + t8 * SUBLANES, SUBLANES), SUBLANES)
            q8 = q_ref[rows8, :]
            b8 = b_ref[rows8, :]
            out = [jnp.zeros((SUBLANES, dv), F32) for _ in range(heads)]
            for r in range(SUBLANES):
                t = t8 * SUBLANES + r
                decay = jnp.exp(jnp.where(s_idx <= t, b8[r:r + 1, :] - b, -jnp.inf))
                w = q8[r:r + 1, :] * decay * k
                for h in range(heads):
                    sc = jnp.sum(w[:, h * dk:(h + 1) * dk], axis=1, keepdims=True)
                    o_t = jnp.sum(sc * v_ref[rows, h * dv:(h + 1) * dv], axis=0, keepdims=True)
                    out[h] = jnp.where(sub == r, o_t, out[h])
            for h in range(heads):
                o_ref[rows8, h * dv:(h + 1) * dv] += out[h]
            return carry

        return lax.fori_loop(0, chunk // SUBLANES, token_group, carry)

    for flag, pred in ((True, factorable), (False, jnp.logical_not(factorable))):
        @pl.when(pred)
        def _():
            if flag:
                kt_ref[...] = (k3 * jnp.exp(-b3)).astype(BF16).reshape(tile, kd)
            lax.fori_loop(0, n_chunks, functools.partial(chunk_step, factorable=flag), 0)

    gn = gn_ref[...]
    r = r_ref[...]
    parts = []
    for h in range(heads):
        vs = slice(h * dv, (h + 1) * dv)
        parts.append(_rms(o_ref[:, vs], gn[:, vs]))
    on = jnp.concatenate(parts, axis=1)
    on = (on * (r * jax.nn.sigmoid(r))).astype(BF16)
    y_ref[...] = x_ref[...] + _dot(on, wo_ref[...])

    @pl.when(t == pl.num_programs(1) - 1)
    def _():
        for h in range(heads):
            sfin_ref[0, h] = st_ref[h].T


def _gla_mix(x, q, k, v, r, g, s0, gn, wo, *, batch, chunk, tile):
    n, d = x.shape
    seq = n // batch
    heads, dk, dv = s0.shape[1:]
    nt = seq // tile
    tri = jnp.tril(jnp.ones((chunk, chunk), F32)).astype(BF16)
    row = lambda w: pl.BlockSpec((tile, w), lambda b, t: (b * nt + t, 0))
    y, sfin = pl.pallas_call(
        functools.partial(_gla_mix_body, heads=heads, chunk=chunk),
        grid=(batch, nt),
        in_specs=[row(d), row(heads * dk), row(heads * dk), row(heads * dv), row(heads * dv),
                  row(heads * dk),
                  pl.BlockSpec((1, heads, dk, dv), lambda b, t: (b, 0, 0, 0)),
                  _const_spec(gn), _const_spec(wo), _const_spec(tri)],
        out_specs=[row(d), pl.BlockSpec((1, heads, dk, dv), lambda b, t: (b, 0, 0, 0))],
        out_shape=[jax.ShapeDtypeStruct((n, d), F32),
                   jax.ShapeDtypeStruct((batch, heads, dk, dv), F32)],
        scratch_shapes=[pltpu.VMEM((heads, dv, dk), F32), pltpu.VMEM((tile, heads * dv), F32),
                        pltpu.VMEM((tile, heads * dk), F32)]
                       + [pltpu.VMEM((tile, heads * dk), BF16)] * 3
                       + [pltpu.VMEM((tile // chunk, SUBLANES, heads * dk), F32)],
        compiler_params=_params("arbitrary", "arbitrary"),
    )(x, q, k, v, r, g, s0, gn, wo, tri)
    return y, sfin


def _sb_prompt_body(bias_ref, q_ref, k_ref, v_ref, tri_ref, o_ref,
                    lb_ref, hi_ref, zb_ref, qh_ref, acc_ref, run_ref, *, blk, hd):
    h2 = pl.program_id(1)
    nq = k_ref.shape[1]
    n_pairs = nq * (nq + 1) // 2
    tri = tri_ref[...]
    lane = lax.broadcasted_iota(jnp.int32, (blk, LANES), 1)
    row = lax.broadcasted_iota(jnp.int32, (blk, blk), 0)
    col = lax.broadcasted_iota(jnp.int32, (blk, blk), 1)
    heads = range(LANES // hd)
    in_head = [(lane >= i * hd) & (lane < (i + 1) * hd) for i in heads]

    q_all = q_ref[0]
    q_lane = lax.broadcasted_iota(jnp.int32, q_all.shape, 1)
    for i in heads:
        b = bias_ref[h2 * len(heads) + i]
        zb_ref[i, 0] = jnp.full((blk, blk), b, F32)
        zb_ref[i, 1] = jnp.where(col < row, b, -jnp.inf)
        qh_ref[i] = jnp.where((q_lane >= i * hd) & (q_lane < (i + 1) * hd), q_all,
                              jnp.zeros_like(q_all))
    lb_ref[...] = jnp.zeros_like(lb_ref)
    hi_ref[...] = jnp.zeros_like(hi_ref)
    acc_ref[...] = jnp.zeros_like(acc_ref)
    run_ref[...] = jnp.zeros_like(run_ref)

    def following(qi, kb):
        step_q = jnp.minimum(qi + 1, nq - 1)
        return jnp.where(kb > 0, qi, step_q), jnp.where(kb > 0, kb - 1, step_q)

    def body(pairs):
        (qf, kf), (qb, kb) = pairs
        hi = [hi_ref[i] for i in heads]
        later = [_dot(hi[i], tri) for i in heads]
        q_rows = pl.ds(pl.multiple_of(qf * blk, blk), blk)
        kt = k_ref[0, kf]
        diag = (kf == qf).astype(jnp.int32)
        z = [_dot(qh_ref[i, q_rows, :], kt) + zb_ref[i, diag] for i in heads]
        vt = v_ref[0, kb]
        for i in heads:
            run = jnp.where(kb == qb, 0.0, run_ref[i])
            a = jnp.exp(lb_ref[i] - later[i] + run).astype(BF16)
            run_ref[i] = run - (later[i][:, 0:1] + hi[i][:, 0:1].astype(F32))
            acc_ref[i] = jnp.where(kb == qb, 0.0, acc_ref[i]) + _dot_nt(a, vt)
        for i in heads:
            s = _softplus(z[i])
            lb_ref[i] = z[i] - s
            hi_ref[i] = s.astype(BF16)
        o = acc_ref[0]
        for i in heads[1:]:
            o = jnp.where(in_head[i], acc_ref[i], o)
        o_ref[0, pl.ds(pl.multiple_of(qb * blk, blk), blk), :] = o.astype(o_ref.dtype)
        return following(qf, kf), (qf, kf)

    def unrolled(_, pairs):
        for _ in range(SB_UNROLL):
            pairs = body(pairs)
        return pairs

    zero = jnp.int32(0)
    steps = n_pairs + 1
    pairs = lax.fori_loop(0, steps // SB_UNROLL, unrolled, ((zero, zero),) * 2)
    for _ in range(steps % SB_UNROLL):
        pairs = body(pairs)


def _sb_prompt(q, kt, vt, bias, *, batch, hd):
    n, d = q.shape
    seq = n // batch
    nq, blk = kt.shape[1], kt.shape[3]
    q3 = q.reshape(batch, seq, d)
    tri = (jnp.arange(blk)[:, None] > jnp.arange(blk)[None, :]).astype(BF16)
    heads = LANES // hd
    q_spec = pl.BlockSpec((1, seq, LANES), lambda b, h: (b, 0, h))
    kv_spec = pl.BlockSpec((1, nq, LANES, blk), lambda b, h: (b, 0, h, 0))
    stage = lambda dt: pltpu.VMEM((heads, blk, blk), dt)
    o = pl.pallas_call(
        functools.partial(_sb_prompt_body, blk=blk, hd=hd),
        grid=(batch, d // LANES),
        in_specs=[pl.BlockSpec(memory_space=pltpu.SMEM), q_spec, kv_spec, kv_spec, _const_spec(tri)],
        out_specs=q_spec,
        out_shape=jax.ShapeDtypeStruct((batch, seq, d), BF16),
        scratch_shapes=[stage(F32), stage(BF16),
                        pltpu.VMEM((heads, 2, blk, blk), F32),
                        pltpu.VMEM((heads, seq, LANES), BF16),
                        pltpu.VMEM((heads, blk, LANES), F32), pltpu.VMEM((heads, blk, 1), F32)],
        compiler_params=_params("parallel", "parallel"),
    )(bias, q3, kt, vt, tri)
    return o.reshape(n, d)


def _sb_sample_body(pt_ref, qbd_ref, bias_ref, kn_ref, vn_ref, tri_ref, *rest, pages, heads, hd):
    k_refs = rest[:pages]
    v_refs = rest[pages:2 * pages]
    o_ref, acc_ref, run_ref, kpad_ref, vpad_ref = rest[2 * pages:]
    p = pl.program_id(1)
    t_new = kn_ref.shape[1]
    rows = qbd_ref.shape[1]
    psz = kpad_ref.shape[0]
    qbd = qbd_ref[0]
    bias = bias_ref[...]
    tri = tri_ref[...]

    def sweep(kps, vps, causal):
        z = [(_dot(qbd, kp) if causal is None else _dot_nt(qbd, kp)) + bias for kp in kps]
        sp = [_softplus(zi) for zi in z]
        lb = [zi - si for zi, si in zip(z, sp)]
        if causal is not None:
            sp = [jnp.where(causal, si, 0.0) for si in sp]
            lb = [jnp.where(causal, li, -jnp.inf) for li in lb]
        both = [_dot(jnp.concatenate(_split_bf16(si), axis=1), tri) for si in sp]
        run = run_ref[...]
        acc = acc_ref[...]
        for li, bi, vp in zip(lb, both, vps):
            a = jnp.exp(li - bi[:, :psz] + run).astype(BF16)
            run = run - bi[:, psz:]
            acc = acc + (_dot_nt(a, vp) if causal is None else _dot(a, vp))
        run_ref[...] = run
        acc_ref[...] = acc

    @pl.when(p == 0)
    def _():
        acc_ref[...] = jnp.zeros_like(acc_ref)
        run_ref[...] = jnp.zeros_like(run_ref)
        kpad_ref[...] = jnp.zeros_like(kpad_ref)
        vpad_ref[...] = jnp.zeros_like(vpad_ref)
        kpad_ref[0:t_new, :] = kn_ref[0]
        vpad_ref[0:t_new, :] = vn_ref[0]
        tq = lax.broadcasted_iota(jnp.int32, (rows, psz), 0) % t_new
        causal = lax.broadcasted_iota(jnp.int32, (rows, psz), 1) < tq
        sweep([kpad_ref[...].astype(BF16)], [vpad_ref[...].astype(BF16)], causal)

    sweep([r[0].astype(BF16) for r in k_refs], [r[0].astype(BF16) for r in v_refs], None)

    @pl.when(p == pl.num_programs(1) - 1)
    def _():
        lane = lax.broadcasted_iota(jnp.int32, (t_new, LANES), 1)
        per_tile = LANES // hd
        for gidx in range(heads // per_tile):
            cols = slice(gidx * LANES, (gidx + 1) * LANES)
            o = acc_ref[gidx * per_tile * t_new:(gidx * per_tile + 1) * t_new, cols]
            for i in range(1, per_tile):
                h = gidx * per_tile + i
                o = jnp.where(lane >= i * hd, acc_ref[h * t_new:(h + 1) * t_new, cols], o)
            o_ref[0, :, cols] = o


def _sb_sample(q, k_new, v_new, bias, cache_k, cache_v, layer, page_table, *, heads, hd):
    db, t_new, d = q.shape
    n_layers, n_phys, psz = cache_k.shape[:3]
    n_pages = page_table.shape[1]
    pages = PAGES_PER_STEP if n_pages % PAGES_PER_STEP == 0 else 1
    ck = cache_k.transpose(0, 1, 3, 4, 2).reshape(n_layers * n_phys, d, psz)
    cv = cache_v.transpose(0, 1, 3, 4, 2).reshape(n_layers * n_phys, d, psz)
    head_of_lane = jnp.arange(d) // hd
    qbd = jnp.where(head_of_lane[None, None, None, :] == jnp.arange(heads)[None, :, None, None],
                    q[:, None, :, :], jnp.zeros((), q.dtype)).reshape(db, heads * t_new, d)
    bias_rows = jnp.broadcast_to(jnp.repeat(bias.astype(F32), t_new)[:, None], (heads * t_new, psz))
    tri = jnp.concatenate([(jnp.arange(psz)[:, None] > jnp.arange(psz)[None, :]).astype(BF16),
                           jnp.ones((psz, psz), BF16)], axis=1)
    tri = jnp.tile(tri, (2, 1))
    base = layer * n_phys

    def page_spec(i):
        return pl.BlockSpec(
            (1, d, psz), lambda b, p, pt: (pt[b, n_pages - 1 - (p * pages + i)] + base, 0, 0))

    per_seq = lambda shape: pl.BlockSpec(shape, lambda b, p, pt: (b, 0, 0))
    const = lambda a: pl.BlockSpec(a.shape, lambda b, p, pt: (0,) * a.ndim)
    grid_spec = pltpu.PrefetchScalarGridSpec(
        num_scalar_prefetch=1,
        grid=(db, n_pages // pages),
        in_specs=[per_seq((1, heads * t_new, d)), const(bias_rows),
                  per_seq((1, t_new, d)), per_seq((1, t_new, d)), const(tri)]
                 + [page_spec(i) for i in range(pages)] * 2,
        out_specs=per_seq((1, t_new, d)),
        scratch_shapes=[pltpu.VMEM((heads * t_new, d), F32), pltpu.VMEM((heads * t_new, psz), F32),
                        pltpu.VMEM((psz, d), F32), pltpu.VMEM((psz, d), F32)],
    )
    return pl.pallas_call(
        functools.partial(_sb_sample_body, pages=pages, heads=heads, hd=hd),
        grid_spec=grid_spec,
        out_shape=jax.ShapeDtypeStruct((db, t_new, d), F32),
        compiler_params=_params("parallel", "arbitrary"),
    )(page_table, qbd, bias_rows, k_new, v_new, tri, *([ck] * pages), *([cv] * pages))


def kernel(x_prompt, x_sample, state_gla, cache_k, cache_v, page_table, p_prompt, p_sample,
           norm_ffn1, ffn1_w_in, ffn1_w_out, norm_mix, gla_w_in, gla_w_a2, gla_b_a, gla_norm,
           gla_w_out, sb_w_qkv, sb_bias, sb_w_out, norm_ffn2, ffn2_w_in, ffn2_w_out, norm_pe,
           pe_w_gate, pe_w_proj, norm_final):
    bp, tp, d = x_prompt.shape
    bs, ts, _ = x_sample.shape
    depth = norm_ffn1.shape[0]
    n_mixers = 2
    gla_heads, gla_dk, gla_dv = state_gla.shape[2:]
    dk_total, dv_total = gla_heads * gla_dk, gla_heads * gla_dv
    rank = gla_w_a2.shape[1]
    sb_heads, sb_hd = cache_k.shape[3:]
    row = lambda a: a.reshape(1, -1)

    groups = [dict(x=x_prompt.reshape(bp * tp, d), p=p_prompt.reshape(depth * bp * tp, -1), batch=bp),
              dict(x=x_sample.reshape(bs * ts, d), p=p_sample.reshape(depth * bs * ts, -1), batch=bs)]
    gla_states = [[], []]
    new_k = [[], []]
    new_v = [[], []]

    for i in range(depth):
        j = i // n_mixers
        w1_in, w1_out = ffn1_w_in[i].astype(BF16), ffn1_w_out[i].astype(BF16)
        w2_in, w2_out = ffn2_w_in[i].astype(BF16), ffn2_w_out[i].astype(BF16)
        wg, wp = pe_w_gate[i].astype(BF16), pe_w_proj[i].astype(BF16)
        if i % n_mixers == 0:
            w_main = gla_w_in[j, :, :2 * dk_total + 2 * dv_total].astype(BF16)
            w_a = jnp.pad(gla_w_in[j, :, 2 * dk_total + 2 * dv_total:],
                          ((0, 0), (0, LANES - rank))).astype(BF16)
            w_a2 = jnp.pad(gla_w_a2[j], ((0, LANES - rank), (0, 0))).astype(BF16)
            w_o = gla_w_out[j].astype(BF16)
        else:
            w_qkv = sb_w_qkv[j].astype(BF16)
            w_o = sb_w_out[j].astype(BF16)

        for gi, grp in enumerate(groups):
            x = grp["x"]
            is_prompt = gi == 0
            (x,) = _row_call(_ffn_body, [x], [row(norm_ffn1[i]), w1_in, w1_out], [(d, F32)], 512)
            if i % n_mixers == 0:
                q, k, v, r, g = _row_call(
                    _gla_proj_body, [x], [row(norm_mix[i]), w_main, w_a, w_a2, row(gla_b_a[j])],
                    [(dk_total, F32), (dk_total, F32), (dv_total, F32), (dv_total, F32), (dk_total, F32)],
                    512, dk_total=dk_total, dv_total=dv_total, q_scale=gla_dk ** -0.5)
                if is_prompt:
                    s0 = jnp.zeros((bp,) + state_gla.shape[2:], F32)
                    x, s_fin = _gla_mix(x, q, k, v, r, g, s0, row(gla_norm[j]), w_o,
                                        batch=bp, chunk=GLA_CHUNK, tile=min(512, tp))
                else:
                    x, s_fin = _gla_mix(x, q, k, v, r, g, state_gla[j].astype(F32), row(gla_norm[j]),
                                        w_o, batch=bs, chunk=ts, tile=ts)
                gla_states[gi].append(s_fin.astype(state_gla.dtype))
            else:
                if is_prompt:
                    q, kt, vt, ktb, vtb = _sb_proj_t(
                        x, row(norm_mix[i]), w_qkv[:, :d], w_qkv[:, d:2 * d].T, w_qkv[:, 2 * d:].T,
                        batch=bp, tm=512, blk=min(SB_BLOCK, tp), q_scale=sb_hd ** -0.5)
                    o = _sb_prompt(q, ktb, vtb, sb_bias[j].astype(F32), batch=bp, hd=sb_hd)
                    k, v = (a.reshape(bp, sb_heads, sb_hd, tp).transpose(0, 3, 1, 2) for a in (kt, vt))
                else:
                    q, k, v = _row_call(
                        _sb_proj_body, [x], [row(norm_mix[i]), w_qkv],
                        [(d, BF16), (d, F32), (d, F32)], 512, d=d, q_scale=sb_hd ** -0.5)
                    o = _sb_sample(q.reshape(bs, ts, d), k.reshape(bs, ts, d), v.reshape(bs, ts, d),
                                   sb_bias[j], cache_k, cache_v, j, page_table,
                                   heads=sb_heads, hd=sb_hd).reshape(bs * ts, d)
                    k, v = (a.reshape(bs, ts, sb_heads, sb_hd) for a in (k, v))
                new_k[gi].append(k.astype(cache_k.dtype))
                new_v[gi].append(v.astype(cache_v.dtype))
            has_mix = i % n_mixers != 0
            (x,) = _row_call(
                _post_mixer_body, [x] + ([o] if has_mix else []) + [(grp["p"], i * x.shape[0])],
                ([w_o] if has_mix else []) + [row(norm_ffn2[i]), w2_in, w2_out, row(norm_pe[i]), wg, wp,
                                              row(norm_final)],
                [(d, F32)], 512, has_mix=has_mix, final=(i == depth - 1))
            grp["x"] = x

    return (groups[0]["x"].reshape(bp, tp, d), groups[1]["x"].reshape(bs, ts, d),
            jnp.stack(gla_states[0]), jnp.stack(gla_states[1]),
            jnp.stack(new_k[0]), jnp.stack(new_v[0]), jnp.stack(new_k[1]), jnp.stack(new_v[1]))
```

```python
import functools

import jax
import jax.numpy as jnp
from jax import lax
from jax.experimental import pallas as pl
from jax.experimental.pallas import tpu as pltpu

F32 = jnp.float32
BF16 = jnp.bfloat16

NORM_EPS = 1e-6
LOG2E = 1.4426950408889634
GLA_TAU = 16.0
GLA_CHUNK = 64
SB_BLOCK = 256
SB_UNROLL = 8
PAGES_PER_STEP = 8
GLA_MAX_CHUNK_DECAY = 80.0
LANES = 128
SUBLANES = 8
VMEM_LIMIT = 56 * 1024 * 1024


def _params(*sem):
    return pltpu.CompilerParams(dimension_semantics=sem, vmem_limit_bytes=VMEM_LIMIT)


def _rms(x, g):
    return x * lax.rsqrt(jnp.mean(x * x, axis=-1, keepdims=True) + NORM_EPS) * g


def _dot(a, b):
    return jnp.dot(a, b, preferred_element_type=F32)


def _dot_nt(a, b):
    return lax.dot_general(a, b, (((1,), (1,)), ((), ())), preferred_element_type=F32)


def _dot_tn(a, b):
    return lax.dot_general(a, b, (((0,), (0,)), ((), ())), preferred_element_type=F32)


def _softplus(z):
    neg_abs = lax.bitcast_convert_type(
        lax.bitcast_convert_type(z, jnp.uint32) | jnp.uint32(0x80000000), F32)
    return jnp.maximum(z, 0.0) + jnp.log(1.0 + jnp.exp(neg_abs))


def _split_bf16(x):
    hi = x.astype(BF16)
    lo = (x - hi.astype(F32)).astype(BF16)
    return hi, lo


def _const_spec(a):
    nd = a.ndim
    return pl.BlockSpec(a.shape, lambda *_: (0,) * nd, pipeline_mode=pl.Buffered(1))


def _row_call(body, row_args, const_args, out_dims, tm, **kw):
    n = row_args[0].shape[0]
    tm = min(tm, n)
    assert n % tm == 0
    row_args = [a if isinstance(a, tuple) else (a, 0) for a in row_args]
    assert all(first % tm == 0 for _, first in row_args)
    in_specs = [pl.BlockSpec((tm, a.shape[1]), lambda i, off=first // tm: (i + off, 0))
                for a, first in row_args]
    row_args = [a for a, _ in row_args]
    in_specs += [_const_spec(a) for a in const_args]
    out_specs = [pl.BlockSpec((tm, d), lambda i: (i, 0)) for d, _ in out_dims]
    out_shape = [jax.ShapeDtypeStruct((n, d), dt) for d, dt in out_dims]
    outs = pl.pallas_call(
        functools.partial(body, **kw),
        grid=(n // tm,),
        in_specs=in_specs,
        out_specs=out_specs,
        out_shape=out_shape,
        compiler_params=_params("parallel"),
    )(*row_args, *const_args)
    return outs


def _ffn(x, g_ref, win_ref, wout_ref):
    xn = _rms(x, g_ref[...]).astype(BF16)
    h = _dot(xn, win_ref[...])
    dff = wout_ref.shape[0]
    gate, up = h[:, :dff], h[:, dff:]
    act = (gate * jax.nn.sigmoid(gate) * up).astype(BF16)
    return x + 0.5 * _dot(act, wout_ref[...])


def _ffn_body(x_ref, g_ref, win_ref, wout_ref, o_ref):
    o_ref[...] = _ffn(x_ref[...], g_ref, win_ref, wout_ref)


def _post_mixer_body(*refs, has_mix, final):
    x_ref, *refs = refs
    x = x_ref[...]
    if has_mix:
        mix_ref, *refs = refs
    p_ref, *refs = refs
    if has_mix:
        wmix_ref, *refs = refs
        x = x + _dot(mix_ref[...].astype(BF16), wmix_ref[...])
    g_ref, win_ref, wout_ref, gpe_ref, wg_ref, wp_ref, gf_ref, o_ref = refs
    x = _ffn(x, g_ref, win_ref, wout_ref)
    xn = _rms(x, gpe_ref[...]).astype(BF16)
    gate = jax.nn.sigmoid(_dot(xn, wg_ref[...]))
    x = x + gate * _dot(p_ref[...].astype(BF16), wp_ref[...])
    if final:
        x = _rms(x, gf_ref[...])
    o_ref[...] = x


def _gla_proj_body(x_ref, g_ref, w_ref, wa_ref, wa2_ref, ba_ref,
                   q_ref, k_ref, v_ref, r_ref, gate_ref, *, dk_total, dv_total, q_scale):
    xn = _rms(x_ref[...], g_ref[...]).astype(BF16)
    y = _dot(xn, w_ref[...])
    q_ref[...] = y[:, :dk_total] * q_scale
    k_ref[...] = y[:, dk_total:2 * dk_total]
    v_ref[...] = y[:, 2 * dk_total:2 * dk_total + dv_total]
    r_ref[...] = y[:, 2 * dk_total + dv_total:]
    a = _dot(xn, wa_ref[...]).astype(BF16)
    ga = _dot(a, wa2_ref[...]) + ba_ref[...]
    gate_ref[...] = (jnp.minimum(ga, 0.0) - jnp.log1p(jnp.exp(-jnp.abs(ga)))) * (1.0 / GLA_TAU)


def _sb_proj_body(x_ref, g_ref, w_ref, q_ref, k_ref, v_ref, *, d, q_scale):
    xn = _rms(x_ref[...], g_ref[...]).astype(BF16)
    y = _dot(xn, w_ref[...])
    q_ref[...] = (y[:, :d] * q_scale).astype(BF16)
    k_ref[...] = y[:, d:2 * d]
    v_ref[...] = y[:, 2 * d:]


def _sb_proj_t_body(x_ref, g_ref, wq_ref, wkt_ref, wvt_ref, q_ref, kt_ref, vt_ref, ktb_ref, vtb_ref,
                    *, q_scale):
    xn = _rms(x_ref[...], g_ref[...]).astype(BF16)
    q_ref[...] = (_dot(xn, wq_ref[...]) * q_scale).astype(BF16)
    blk = ktb_ref.shape[3]
    for w_ref, t_ref, tb_ref in ((wkt_ref, kt_ref, ktb_ref), (wvt_ref, vt_ref, vtb_ref)):
        yt = _dot_nt(w_ref[...], xn)
        t_ref[0] = yt
        for j in range(tb_ref.shape[1]):
            tb_ref[0, j] = yt[:, j * blk:(j + 1) * blk].astype(BF16)


def _sb_proj_t(x, g, wq, wkt, wvt, *, batch, tm, blk, q_scale):
    n, d = x.shape
    seq = n // batch
    tm = min(tm, seq)
    nt = seq // tm
    row = pl.BlockSpec((tm, d), lambda b, t: (b * nt + t, 0))
    tr = pl.BlockSpec((1, d, tm), lambda b, t: (b, 0, t))
    trb = pl.BlockSpec((1, tm // blk, d, blk), lambda b, t: (b, t, 0, 0))
    return pl.pallas_call(
        functools.partial(_sb_proj_t_body, q_scale=q_scale),
        grid=(batch, nt),
        in_specs=[row, _const_spec(g), _const_spec(wq), _const_spec(wkt), _const_spec(wvt)],
        out_specs=[row, tr, tr, trb, trb],
        out_shape=[jax.ShapeDtypeStruct((n, d), BF16),
                   jax.ShapeDtypeStruct((batch, d, seq), F32),
                   jax.ShapeDtypeStruct((batch, d, seq), F32),
                   jax.ShapeDtypeStruct((batch, seq // blk, d, blk), BF16),
                   jax.ShapeDtypeStruct((batch, seq // blk, d, blk), BF16)],
        compiler_params=_params("parallel", "parallel"),
    )(x, g, wq, wkt, wvt)


def _gla_mix_body(x_ref, q_ref, k_ref, v_ref, r_ref, g_ref, s0_ref, gn_ref, wo_ref, tri_ref,
                  y_ref, sfin_ref, st_ref, o_ref, b_ref, qt_ref, kt_ref, ks_ref, dec_ref,
                  *, heads, chunk):
    t = pl.program_id(1)
    tile = x_ref.shape[0]
    dk = q_ref.shape[1] // heads
    dv = v_ref.shape[1] // heads

    @pl.when(t == 0)
    def _():
        for h in range(heads):
            st_ref[h] = s0_ref[0, h].T

    tri = tri_ref[...]
    causal = (lax.broadcasted_iota(jnp.int32, (chunk, chunk), 1)
              <= lax.broadcasted_iota(jnp.int32, (chunk, chunk), 0))

    n_chunks = tile // chunk
    kd = heads * dk

    for c in range(n_chunks):
        rows = slice(c * chunk, (c + 1) * chunk)
        g_hi, g_lo = _split_bf16(g_ref[rows, :])
        b_ref[rows, :] = _dot(tri, g_hi) + _dot(tri, g_lo)
    b3 = b_ref[...].reshape(n_chunks, chunk, kd)
    b_last = b3[:, chunk - 1:chunk, :]
    q3 = q_ref[...].reshape(n_chunks, chunk, kd)
    k3 = k_ref[...].reshape(n_chunks, chunk, kd)
    qt_ref[...] = (q3 * jnp.exp(b3)).astype(BF16).reshape(tile, kd)
    ks_ref[...] = (k3 * jnp.exp(b_last - b3)).astype(BF16).reshape(tile, kd)
    dec_ref[...] = jnp.broadcast_to(jnp.exp(b_last), (n_chunks, SUBLANES, kd))
    factorable = jnp.min(b_last) >= -GLA_MAX_CHUNK_DECAY

    def chunk_step(c, carry, *, factorable):
        rows = pl.ds(pl.multiple_of(c * chunk, chunk), chunk)
        q_t = qt_ref[rows, :]
        k_s = ks_ref[rows, :]
        v = v_ref[rows, :].astype(BF16)
        s_t = [st_ref[h] for h in range(heads)]
        hk = [slice(h * dk, (h + 1) * dk) for h in range(heads)]
        hv = [slice(h * dv, (h + 1) * dv) for h in range(heads)]
        o = [_dot_nt(q_t[:, hk[h]], s_t[h].astype(BF16)) for h in range(heads)]
        if factorable:
            k_t = kt_ref[rows, :]
            sc = [jnp.where(causal, _dot_nt(q_t[:, hk[h]], k_t[:, hk[h]]), 0.0).astype(BF16)
                  for h in range(heads)]
        upd = [_dot_tn(v[:, hv[h]], k_s[:, hk[h]]) for h in range(heads)]
        if factorable:
            o = [o[h] + _dot(sc[h], v[:, hv[h]]) for h in range(heads)]
        dec = dec_ref[c]
        for h in range(heads):
            o_ref[rows, hv[h]] = o[h]
            st_ref[h] = s_t[h] * dec[0:1, hk[h]] + upd[h]
        if factorable:
            return carry

        b = b_ref[rows, :]
        k = k_ref[rows, :]
        s_idx = lax.broadcasted_iota(jnp.int32, (chunk, 1), 0)
        sub = lax.broadcasted_iota(jnp.int32, (SUBLANES, 1), 0)

        def token_group(t8, carry):
            rows8 = pl.ds(pl.multiple_of(c * chunk + t8 * SUBLANES, SUBLANES), SUBLANES)
            q8 = q_ref[rows8, :]
            b8 = b_ref[rows8, :]
            out = [jnp.zeros((SUBLANES, dv), F32) for _ in range(heads)]
            for r in range(SUBLANES):
                t = t8 * SUBLANES + r
                decay = jnp.exp(jnp.where(s_idx <= t, b8[r:r + 1, :] - b, -jnp.inf))
                w = q8[r:r + 1, :] * decay * k
                for h in range(heads):
                    sc = jnp.sum(w[:, h * dk:(h + 1) * dk], axis=1, keepdims=True)
                    o_t = jnp.sum(sc * v_ref[rows, h * dv:(h + 1) * dv], axis=0, keepdims=True)
                    out[h] = jnp.where(sub == r, o_t, out[h])
            for h in range(heads):
                o_ref[rows8, h * dv:(h + 1) * dv] += out[h]
            return carry

        return lax.fori_loop(0, chunk // SUBLANES, token_group, carry)

    for flag, pred in ((True, factorable), (False, jnp.logical_not(factorable))):
        @pl.when(pred)
        def _():
            if flag:
                kt_ref[...] = (k3 * jnp.exp(-b3)).astype(BF16).reshape(tile, kd)
            lax.fori_loop(0, n_chunks, functools.partial(chunk_step, factorable=flag), 0)

    gn = gn_ref[...]
    r = r_ref[...]
    parts = []
    for h in range(heads):
        vs = slice(h * dv, (h + 1) * dv)
        parts.append(_rms(o_ref[:, vs], gn[:, vs]))
    on = jnp.concatenate(parts, axis=1)
    on = (on * (r * jax.nn.sigmoid(r))).astype(BF16)
    y_ref[...] = x_ref[...] + _dot(on, wo_ref[...])

    @pl.when(t == pl.num_programs(1) - 1)
    def _():
        for h in range(heads):
            sfin_ref[0, h] = st_ref[h].T


def _gla_mix(x, q, k, v, r, g, s0, gn, wo, *, batch, chunk, tile):
    n, d = x.shape
    seq = n // batch
    heads, dk, dv = s0.shape[1:]
    nt = seq // tile
    tri = jnp.tril(jnp.ones((chunk, chunk), F32)).astype(BF16)
    row = lambda w: pl.BlockSpec((tile, w), lambda b, t: (b * nt + t, 0))
    y, sfin = pl.pallas_call(
        functools.partial(_gla_mix_body, heads=heads, chunk=chunk),
        grid=(batch, nt),
        in_specs=[row(d), row(heads * dk), row(heads * dk), row(heads * dv), row(heads * dv),
                  row(heads * dk),
                  pl.BlockSpec((1, heads, dk, dv), lambda b, t: (b, 0, 0, 0)),
                  _const_spec(gn), _const_spec(wo), _const_spec(tri)],
        out_specs=[row(d), pl.BlockSpec((1, heads, dk, dv), lambda b, t: (b, 0, 0, 0))],
        out_shape=[jax.ShapeDtypeStruct((n, d), F32),
                   jax.ShapeDtypeStruct((batch, heads, dk, dv), F32)],
        scratch_shapes=[pltpu.VMEM((heads, dv, dk), F32), pltpu.VMEM((tile, heads * dv), F32),
                        pltpu.VMEM((tile, heads * dk), F32)]
                       + [pltpu.VMEM((tile, heads * dk), BF16)] * 3
                       + [pltpu.VMEM((tile // chunk, SUBLANES, heads * dk), F32)],
        compiler_params=_params("arbitrary", "arbitrary"),
    )(x, q, k, v, r, g, s0, gn, wo, tri)
    return y, sfin


def _sb_prompt_body(bias_ref, q_ref, k_ref, v_ref, tri_ref, o_ref,
                    lb_ref, hi_ref, zb_ref, qh_ref, acc_ref, run_ref, *, blk, hd):
    h2 = pl.program_id(1)
    nq = k_ref.shape[1]
    n_pairs = nq * (nq + 1) // 2
    tri = tri_ref[...]
    lane = lax.broadcasted_iota(jnp.int32, (blk, LANES), 1)
    row = lax.broadcasted_iota(jnp.int32, (blk, blk), 0)
    col = lax.broadcasted_iota(jnp.int32, (blk, blk), 1)
    heads = range(LANES // hd)
    in_head = [(lane >= i * hd) & (lane < (i + 1) * hd) for i in heads]

    q_all = q_ref[0]
    q_lane = lax.broadcasted_iota(jnp.int32, q_all.shape, 1)
    for i in heads:
        b = bias_ref[h2 * len(heads) + i]
        zb_ref[i, 0] = jnp.full((blk, blk), b, F32)
        zb_ref[i, 1] = jnp.where(col < row, b, -jnp.inf)
        qh_ref[i] = jnp.where((q_lane >= i * hd) & (q_lane < (i + 1) * hd), q_all,
                              jnp.zeros_like(q_all))
    lb_ref[...] = jnp.zeros_like(lb_ref)
    hi_ref[...] = jnp.zeros_like(hi_ref)
    acc_ref[...] = jnp.zeros_like(acc_ref)
    run_ref[...] = jnp.zeros_like(run_ref)

    def following(qi, kb):
        step_q = jnp.minimum(qi + 1, nq - 1)
        return jnp.where(kb > 0, qi, step_q), jnp.where(kb > 0, kb - 1, step_q)

    def body(pairs):
        (qf, kf), (qb, kb) = pairs
        q_rows = pl.ds(pl.multiple_of(qf * blk, blk), blk)
        kt = k_ref[0, kf]
        vt = v_ref[0, kb]
        diag = (kf == qf).astype(jnp.int32)

        def logits(i):
            return _dot(qh_ref[i, q_rows, :], kt) + zb_ref[i, diag]

        def back(i, hi, later):
            run = jnp.where(kb == qb, 0.0, run_ref[i])
            a = jnp.exp(lb_ref[i] - later + run).astype(BF16)
            run_ref[i] = run - (later[:, 0:1] + hi[:, 0:1].astype(F32))
            acc_ref[i] = jnp.where(kb == qb, 0.0, acc_ref[i]) + _dot_nt(a, vt)

        def soft(i, z):
            s = _softplus(z)
            lb_ref[i] = z - s
            hi_ref[i] = s.astype(BF16)

        hi = [hi_ref[i] for i in heads]
        later = [_dot(hi[i], tri) for i in heads]
        z = logits(0)
        for i in heads:
            back(i, hi[i], later[i])
            if i > 0:
                z = logits(i)
            soft(i, z)
        o = acc_ref[0]
        for i in heads[1:]:
            o = jnp.where(in_head[i], acc_ref[i], o)
        o_ref[0, pl.ds(pl.multiple_of(qb * blk, blk), blk), :] = o.astype(o_ref.dtype)
        return following(qf, kf), (qf, kf)

    def unrolled(_, pairs):
        for _ in range(SB_UNROLL):
            pairs = body(pairs)
        return pairs

    zero = jnp.int32(0)
    steps = n_pairs + 1
    pairs = lax.fori_loop(0, steps // SB_UNROLL, unrolled, ((zero, zero),) * 2)
    for _ in range(steps % SB_UNROLL):
        pairs = body(pairs)


def _sb_prompt(q, kt, vt, bias, *, batch, hd):
    n, d = q.shape
    seq = n // batch
    nq, blk = kt.shape[1], kt.shape[3]
    q3 = q.reshape(batch, seq, d)
    tri = (jnp.arange(blk)[:, None] > jnp.arange(blk)[None, :]).astype(BF16)
    heads = LANES // hd
    q_spec = pl.BlockSpec((1, seq, LANES), lambda b, h: (b, 0, h))
    kv_spec = pl.BlockSpec((1, nq, LANES, blk), lambda b, h: (b, 0, h, 0))
    stage = lambda dt: pltpu.VMEM((heads, blk, blk), dt)
    o = pl.pallas_call(
        functools.partial(_sb_prompt_body, blk=blk, hd=hd),
        grid=(batch, d // LANES),
        in_specs=[pl.BlockSpec(memory_space=pltpu.SMEM), q_spec, kv_spec, kv_spec, _const_spec(tri)],
        out_specs=q_spec,
        out_shape=jax.ShapeDtypeStruct((batch, seq, d), BF16),
        scratch_shapes=[stage(F32), stage(BF16),
                        pltpu.VMEM((heads, 2, blk, blk), F32),
                        pltpu.VMEM((heads, seq, LANES), BF16),
                        pltpu.VMEM((heads, blk, LANES), F32), pltpu.VMEM((heads, blk, 1), F32)],
        compiler_params=_params("parallel", "parallel"),
    )(bias, q3, kt, vt, tri)
    return o.reshape(n, d)


def _sb_sample_body(pt_ref, qbd_ref, bias_ref, kn_ref, vn_ref, tri_ref, *rest, pages, heads, hd):
    k_refs = rest[:pages]
    v_refs = rest[pages:2 * pages]
    o_ref, acc_ref, run_ref, kpad_ref, vpad_ref = rest[2 * pages:]
    p = pl.program_id(1)
    t_new = kn_ref.shape[1]
    rows = qbd_ref.shape[1]
    psz = kpad_ref.shape[0]
    qbd = qbd_ref[0]
    bias = bias_ref[...]
    tri = tri_ref[...]

    def sweep(kps, vps, causal):
        z = [(_dot(qbd, kp) if causal is None else _dot_nt(qbd, kp)) + bias for kp in kps]
        sp = [_softplus(zi) for zi in z]
        lb = [zi - si for zi, si in zip(z, sp)]
        if causal is not None:
            sp = [jnp.where(causal, si, 0.0) for si in sp]
            lb = [jnp.where(causal, li, -jnp.inf) for li in lb]
        both = [_dot(jnp.concatenate(_split_bf16(si), axis=1), tri) for si in sp]
        run = run_ref[...]
        acc = acc_ref[...]
        for li, bi, vp in zip(lb, both, vps):
            a = jnp.exp(li - bi[:, :psz] + run).astype(BF16)
            run = run - bi[:, psz:]
            acc = acc + (_dot_nt(a, vp) if causal is None else _dot(a, vp))
        run_ref[...] = run
        acc_ref[...] = acc

    @pl.when(p == 0)
    def _():
        acc_ref[...] = jnp.zeros_like(acc_ref)
        run_ref[...] = jnp.zeros_like(run_ref)
        kpad_ref[...] = jnp.zeros_like(kpad_ref)
        vpad_ref[...] = jnp.zeros_like(vpad_ref)
        kpad_ref[0:t_new, :] = kn_ref[0]
        vpad_ref[0:t_new, :] = vn_ref[0]
        tq = lax.broadcasted_iota(jnp.int32, (rows, psz), 0) % t_new
        causal = lax.broadcasted_iota(jnp.int32, (rows, psz), 1) < tq
        sweep([kpad_ref[...].astype(BF16)], [vpad_ref[...].astype(BF16)], causal)

    sweep([r[0].astype(BF16) for r in k_refs], [r[0].astype(BF16) for r in v_refs], None)

    @pl.when(p == pl.num_programs(1) - 1)
    def _():
        lane = lax.broadcasted_iota(jnp.int32, (t_new, LANES), 1)
        per_tile = LANES // hd
        for gidx in range(heads // per_tile):
            cols = slice(gidx * LANES, (gidx + 1) * LANES)
            o = acc_ref[gidx * per_tile * t_new:(gidx * per_tile + 1) * t_new, cols]
            for i in range(1, per_tile):
                h = gidx * per_tile + i
                o = jnp.where(lane >= i * hd, acc_ref[h * t_new:(h + 1) * t_new, cols], o)
            o_ref[0, :, cols] = o


def _sb_sample(q, k_new, v_new, bias, cache_k, cache_v, layer, page_table, *, heads, hd):
    db, t_new, d = q.shape
    n_layers, n_phys, psz = cache_k.shape[:3]
    n_pages = page_table.shape[1]
    pages = PAGES_PER_STEP if n_pages % PAGES_PER_STEP == 0 else 1
    ck = cache_k.transpose(0, 1, 3, 4, 2).reshape(n_layers * n_phys, d, psz)
    cv = cache_v.transpose(0, 1, 3, 4, 2).reshape(n_layers * n_phys, d, psz)
    head_of_lane = jnp.arange(d) // hd
    qbd = jnp.where(head_of_lane[None, None, None, :] == jnp.arange(heads)[None, :, None, None],
                    q[:, None, :, :], jnp.zeros((), q.dtype)).reshape(db, heads * t_new, d)
    bias_rows = jnp.broadcast_to(jnp.repeat(bias.astype(F32), t_new)[:, None], (heads * t_new, psz))
    tri = jnp.concatenate([(jnp.arange(psz)[:, None] > jnp.arange(psz)[None, :]).astype(BF16),
                           jnp.ones((psz, psz), BF16)], axis=1)
    tri = jnp.tile(tri, (2, 1))
    base = layer * n_phys

    def page_spec(i):
        return pl.BlockSpec(
            (1, d, psz), lambda b, p, pt: (pt[b, n_pages - 1 - (p * pages + i)] + base, 0, 0))

    per_seq = lambda shape: pl.BlockSpec(shape, lambda b, p, pt: (b, 0, 0))
    const = lambda a: pl.BlockSpec(a.shape, lambda b, p, pt: (0,) * a.ndim)
    grid_spec = pltpu.PrefetchScalarGridSpec(
        num_scalar_prefetch=1,
        grid=(db, n_pages // pages),
        in_specs=[per_seq((1, heads * t_new, d)), const(bias_rows),
                  per_seq((1, t_new, d)), per_seq((1, t_new, d)), const(tri)]
                 + [page_spec(i) for i in range(pages)] * 2,
        out_specs=per_seq((1, t_new, d)),
        scratch_shapes=[pltpu.VMEM((heads * t_new, d), F32), pltpu.VMEM((heads * t_new, psz), F32),
                        pltpu.VMEM((psz, d), F32), pltpu.VMEM((psz, d), F32)],
    )
    return pl.pallas_call(
        functools.partial(_sb_sample_body, pages=pages, heads=heads, hd=hd),
        grid_spec=grid_spec,
        out_shape=jax.ShapeDtypeStruct((db, t_new, d), F32),
        compiler_params=_params("parallel", "arbitrary"),
    )(page_table, qbd, bias_rows, k_new, v_new, tri, *([ck] * pages), *([cv] * pages))


def kernel(x_prompt, x_sample, state_gla, cache_k, cache_v, page_table, p_prompt, p_sample,
           norm_ffn1, ffn1_w_in, ffn1_w_out, norm_mix, gla_w_in, gla_w_a2, gla_b_a, gla_norm,
           gla_w_out, sb_w_qkv, sb_bias, sb_w_out, norm_ffn2, ffn2_w_in, ffn2_w_out, norm_pe,
           pe_w_gate, pe_w_proj, norm_final):
    bp, tp, d = x_prompt.shape
    bs, ts, _ = x_sample.shape
    depth = norm_ffn1.shape[0]
    n_mixers = 2
    gla_heads, gla_dk, gla_dv = state_gla.shape[2:]
    dk_total, dv_total = gla_heads * gla_dk, gla_heads * gla_dv
    rank = gla_w_a2.shape[1]
    sb_heads, sb_hd = cache_k.shape[3:]
    row = lambda a: a.reshape(1, -1)

    groups = [dict(x=x_prompt.reshape(bp * tp, d), p=p_prompt.reshape(depth * bp * tp, -1), batch=bp),
              dict(x=x_sample.reshape(bs * ts, d), p=p_sample.reshape(depth * bs * ts, -1), batch=bs)]
    gla_states = [[], []]
    new_k = [[], []]
    new_v = [[], []]

    for i in range(depth):
        j = i // n_mixers
        w1_in, w1_out = ffn1_w_in[i].astype(BF16), ffn1_w_out[i].astype(BF16)
        w2_in, w2_out = ffn2_w_in[i].astype(BF16), ffn2_w_out[i].astype(BF16)
        wg, wp = pe_w_gate[i].astype(BF16), pe_w_proj[i].astype(BF16)
        if i % n_mixers == 0:
            w_main = gla_w_in[j, :, :2 * dk_total + 2 * dv_total].astype(BF16)
            w_a = jnp.pad(gla_w_in[j, :, 2 * dk_total + 2 * dv_total:],
                          ((0, 0), (0, LANES - rank))).astype(BF16)
            w_a2 = jnp.pad(gla_w_a2[j], ((0, LANES - rank), (0, 0))).astype(BF16)
            w_o = gla_w_out[j].astype(BF16)
        else:
            w_qkv = sb_w_qkv[j].astype(BF16)
            w_o = sb_w_out[j].astype(BF16)

        for gi, grp in enumerate(groups):
            x = grp["x"]
            is_prompt = gi == 0
            (x,) = _row_call(_ffn_body, [x], [row(norm_ffn1[i]), w1_in, w1_out], [(d, F32)], 512)
            if i % n_mixers == 0:
                q, k, v, r, g = _row_call(
                    _gla_proj_body, [x], [row(norm_mix[i]), w_main, w_a, w_a2, row(gla_b_a[j])],
                    [(dk_total, F32), (dk_total, F32), (dv_total, F32), (dv_total, F32), (dk_total, F32)],
                    512, dk_total=dk_total, dv_total=dv_total, q_scale=gla_dk ** -0.5)
                if is_prompt:
                    s0 = jnp.zeros((bp,) + state_gla.shape[2:], F32)
                    x, s_fin = _gla_mix(x, q, k, v, r, g, s0, row(gla_norm[j]), w_o,
                                        batch=bp, chunk=GLA_CHUNK, tile=min(512, tp))
                else:
                    x, s_fin = _gla_mix(x, q, k, v, r, g, state_gla[j].astype(F32), row(gla_norm[j]),
                                        w_o, batch=bs, chunk=ts, tile=ts)
                gla_states[gi].append(s_fin.astype(state_gla.dtype))
            else:
                if is_prompt:
                    q, kt, vt, ktb, vtb = _sb_proj_t(
                        x, row(norm_mix[i]), w_qkv[:, :d], w_qkv[:, d:2 * d].T, w_qkv[:, 2 * d:].T,
                        batch=bp, tm=512, blk=min(SB_BLOCK, tp), q_scale=sb_hd ** -0.5)
                    o = _sb_prompt(q, ktb, vtb, sb_bias[j].astype(F32), batch=bp, hd=sb_hd)
                    k, v = (a.reshape(bp, sb_heads, sb_hd, tp).transpose(0, 3, 1, 2) for a in (kt, vt))
                else:
                    q, k, v = _row_call(
                        _sb_proj_body, [x], [row(norm_mix[i]), w_qkv],
                        [(d, BF16), (d, F32), (d, F32)], 512, d=d, q_scale=sb_hd ** -0.5)
                    o = _sb_sample(q.reshape(bs, ts, d), k.reshape(bs, ts, d), v.reshape(bs, ts, d),
                                   sb_bias[j], cache_k, cache_v, j, page_table,
                                   heads=sb_heads, hd=sb_hd).reshape(bs * ts, d)
                    k, v = (a.reshape(bs, ts, sb_heads, sb_hd) for a in (k, v))
                new_k[gi].append(k.astype(cache_k.dtype))
                new_v[gi].append(v.astype(cache_v.dtype))
            has_mix = i % n_mixers != 0
            (x,) = _row_call(
                _post_mixer_body, [x] + ([o] if has_mix else []) + [(grp["p"], i * x.shape[0])],
                ([w_o] if has_mix else []) + [row(norm_ffn2[i]), w2_in, w2_out, row(norm_pe[i]), wg, wp,
                                              row(norm_final)],
                [(d, F32)], 512, has_mix=has_mix, final=(i == depth - 1))
            grp["x"] = x

    return (groups[0]["x"].reshape(bp, tp, d), groups[1]["x"].reshape(bs, ts, d),
            jnp.stack(gla_states[0]), jnp.stack(gla_states[1]),
            jnp.stack(new_k[0]), jnp.stack(new_v[0]), jnp.stack(new_k[1]), jnp.stack(new_v[1]))
```

```python
import functools

import jax
import jax.numpy as jnp
from jax import lax
from jax.experimental import pallas as pl
from jax.experimental.pallas import tpu as pltpu

F32 = jnp.float32
BF16 = jnp.bfloat16

NORM_EPS = 1e-6
LOG2E = 1.4426950408889634
GLA_TAU = 16.0
GLA_CHUNK = 64
SB_BLOCK = 256
SB_UNROLL = 17
PAGES_PER_STEP = 16
GLA_MAX_CHUNK_DECAY = 80.0
LANES = 128
SUBLANES = 8
VMEM_LIMIT = 56 * 1024 * 1024


def _params(*sem):
    return pltpu.CompilerParams(dimension_semantics=sem, vmem_limit_bytes=VMEM_LIMIT)


def _rms(x, g):
    return x * lax.rsqrt(jnp.mean(x * x, axis=-1, keepdims=True) + NORM_EPS) * g


def _dot(a, b):
    return jnp.dot(a, b, preferred_element_type=F32)


def _dot_nt(a, b):
    return lax.dot_general(a, b, (((1,), (1,)), ((), ())), preferred_element_type=F32)


def _dot_tn(a, b):
    return lax.dot_general(a, b, (((0,), (0,)), ((), ())), preferred_element_type=F32)


def _softplus(z):
    neg_abs = lax.bitcast_convert_type(
        lax.bitcast_convert_type(z, jnp.uint32) | jnp.uint32(0x80000000), F32)
    return jnp.maximum(z, 0.0) + jnp.log(1.0 + jnp.exp(neg_abs))


def _split_bf16(x):
    hi = x.astype(BF16)
    lo = (x - hi.astype(F32)).astype(BF16)
    return hi, lo


def _const_spec(a):
    nd = a.ndim
    return pl.BlockSpec(a.shape, lambda *_: (0,) * nd, pipeline_mode=pl.Buffered(1))


def _row_call(body, row_args, const_args, out_dims, tm, **kw):
    n = row_args[0].shape[0]
    tm = min(tm, n)
    assert n % tm == 0
    row_args = [a if isinstance(a, tuple) else (a, 0) for a in row_args]
    assert all(first % tm == 0 for _, first in row_args)
    in_specs = [pl.BlockSpec((tm, a.shape[1]), lambda i, off=first // tm: (i + off, 0))
                for a, first in row_args]
    row_args = [a for a, _ in row_args]
    in_specs += [_const_spec(a) for a in const_args]
    out_specs = [pl.BlockSpec((tm, d), lambda i: (i, 0)) for d, _ in out_dims]
    out_shape = [jax.ShapeDtypeStruct((n, d), dt) for d, dt in out_dims]
    outs = pl.pallas_call(
        functools.partial(body, **kw),
        grid=(n // tm,),
        in_specs=in_specs,
        out_specs=out_specs,
        out_shape=out_shape,
        compiler_params=_params("parallel"),
    )(*row_args, *const_args)
    return outs


def _ffn(x, g_ref, win_ref, wout_ref):
    xn = _rms(x, g_ref[...]).astype(BF16)
    h = _dot(xn, win_ref[...])
    dff = wout_ref.shape[0]
    gate, up = h[:, :dff], h[:, dff:]
    act = (gate * jax.nn.sigmoid(gate) * up).astype(BF16)
    return x + 0.5 * _dot(act, wout_ref[...])


def _ffn_body(x_ref, g_ref, win_ref, wout_ref, o_ref):
    o_ref[...] = _ffn(x_ref[...], g_ref, win_ref, wout_ref)


def _post_mixer_body(*refs, has_mix, final):
    x_ref, *refs = refs
    x = x_ref[...]
    if has_mix:
        mix_ref, *refs = refs
    p_ref, *refs = refs
    if has_mix:
        wmix_ref, *refs = refs
        x = x + _dot(mix_ref[...].astype(BF16), wmix_ref[...])
    g_ref, win_ref, wout_ref, gpe_ref, wg_ref, wp_ref, gf_ref, o_ref = refs
    x = _ffn(x, g_ref, win_ref, wout_ref)
    xn = _rms(x, gpe_ref[...]).astype(BF16)
    gate = jax.nn.sigmoid(_dot(xn, wg_ref[...]))
    x = x + gate * _dot(p_ref[...].astype(BF16), wp_ref[...])
    if final:
        x = _rms(x, gf_ref[...])
    o_ref[...] = x


def _gla_proj_body(x_ref, g_ref, w_ref, wa_ref, wa2_ref, ba_ref,
                   q_ref, k_ref, v_ref, r_ref, gate_ref, *, dk_total, dv_total, q_scale):
    xn = _rms(x_ref[...], g_ref[...]).astype(BF16)
    y = _dot(xn, w_ref[...])
    q_ref[...] = y[:, :dk_total] * q_scale
    k_ref[...] = y[:, dk_total:2 * dk_total]
    v_ref[...] = y[:, 2 * dk_total:2 * dk_total + dv_total]
    r_ref[...] = y[:, 2 * dk_total + dv_total:]
    a = _dot(xn, wa_ref[...]).astype(BF16)
    ga = _dot(a, wa2_ref[...]) + ba_ref[...]
    gate_ref[...] = (jnp.minimum(ga, 0.0) - jnp.log1p(jnp.exp(-jnp.abs(ga)))) * (1.0 / GLA_TAU)


def _sb_proj_body(x_ref, g_ref, w_ref, q_ref, k_ref, v_ref, *, d, q_scale):
    xn = _rms(x_ref[...], g_ref[...]).astype(BF16)
    y = _dot(xn, w_ref[...])
    q_ref[...] = (y[:, :d] * q_scale).astype(BF16)
    k_ref[...] = y[:, d:2 * d]
    v_ref[...] = y[:, 2 * d:]


def _sb_proj_t_body(x_ref, g_ref, wq_ref, wkt_ref, wvt_ref, q_ref, kt_ref, vt_ref, ktb_ref, vtb_ref,
                    *, q_scale):
    xn = _rms(x_ref[...], g_ref[...]).astype(BF16)
    q_ref[...] = (_dot(xn, wq_ref[...]) * q_scale).astype(BF16)
    blk = ktb_ref.shape[3]
    for w_ref, t_ref, tb_ref in ((wkt_ref, kt_ref, ktb_ref), (wvt_ref, vt_ref, vtb_ref)):
        yt = _dot_nt(w_ref[...], xn)
        t_ref[0] = yt
        for j in range(tb_ref.shape[1]):
            tb_ref[0, j] = yt[:, j * blk:(j + 1) * blk].astype(BF16)


def _sb_proj_t(x, g, wq, wkt, wvt, *, batch, tm, blk, q_scale):
    n, d = x.shape
    seq = n // batch
    tm = min(tm, seq)
    nt = seq // tm
    row = pl.BlockSpec((tm, d), lambda b, t: (b * nt + t, 0))
    tr = pl.BlockSpec((1, d, tm), lambda b, t: (b, 0, t))
    trb = pl.BlockSpec((1, tm // blk, d, blk), lambda b, t: (b, t, 0, 0))
    return pl.pallas_call(
        functools.partial(_sb_proj_t_body, q_scale=q_scale),
        grid=(batch, nt),
        in_specs=[row, _const_spec(g), _const_spec(wq), _const_spec(wkt), _const_spec(wvt)],
        out_specs=[row, tr, tr, trb, trb],
        out_shape=[jax.ShapeDtypeStruct((n, d), BF16),
                   jax.ShapeDtypeStruct((batch, d, seq), F32),
                   jax.ShapeDtypeStruct((batch, d, seq), F32),
                   jax.ShapeDtypeStruct((batch, seq // blk, d, blk), BF16),
                   jax.ShapeDtypeStruct((batch, seq // blk, d, blk), BF16)],
        compiler_params=_params("parallel", "parallel"),
    )(x, g, wq, wkt, wvt)


def _gla_mix_body(x_ref, q_ref, k_ref, v_ref, r_ref, g_ref, s0_ref, gn_ref, wo_ref, tri_ref,
                  y_ref, sfin_ref, st_ref, o_ref, b_ref, qt_ref, kt_ref, ks_ref, dec_ref,
                  *, heads, chunk):
    t = pl.program_id(1)
    tile = x_ref.shape[0]
    dk = q_ref.shape[1] // heads
    dv = v_ref.shape[1] // heads

    @pl.when(t == 0)
    def _():
        for h in range(heads):
            st_ref[h] = s0_ref[0, h].T

    tri = tri_ref[...]
    causal = (lax.broadcasted_iota(jnp.int32, (chunk, chunk), 1)
              <= lax.broadcasted_iota(jnp.int32, (chunk, chunk), 0))

    n_chunks = tile // chunk
    kd = heads * dk

    for c in range(n_chunks):
        rows = slice(c * chunk, (c + 1) * chunk)
        g_hi, g_lo = _split_bf16(g_ref[rows, :])
        b_ref[rows, :] = _dot(tri, g_hi) + _dot(tri, g_lo)
    b3 = b_ref[...].reshape(n_chunks, chunk, kd)
    b_last = b3[:, chunk - 1:chunk, :]
    q3 = q_ref[...].reshape(n_chunks, chunk, kd)
    k3 = k_ref[...].reshape(n_chunks, chunk, kd)
    qt_ref[...] = (q3 * jnp.exp(b3)).astype(BF16).reshape(tile, kd)
    ks_ref[...] = (k3 * jnp.exp(b_last - b3)).astype(BF16).reshape(tile, kd)
    dec_ref[...] = jnp.broadcast_to(jnp.exp(b_last), (n_chunks, SUBLANES, kd))
    factorable = jnp.min(b_last) >= -GLA_MAX_CHUNK_DECAY

    def chunk_step(c, carry, *, factorable):
        rows = pl.ds(pl.multiple_of(c * chunk, chunk), chunk)
        q_t = qt_ref[rows, :]
        k_s = ks_ref[rows, :]
        v = v_ref[rows, :].astype(BF16)
        s_t = [st_ref[h] for h in range(heads)]
        hk = [slice(h * dk, (h + 1) * dk) for h in range(heads)]
        hv = [slice(h * dv, (h + 1) * dv) for h in range(heads)]
        o = [_dot_nt(q_t[:, hk[h]], s_t[h].astype(BF16)) for h in range(heads)]
        if factorable:
            k_t = kt_ref[rows, :]
            sc = [jnp.where(causal, _dot_nt(q_t[:, hk[h]], k_t[:, hk[h]]), 0.0).astype(BF16)
                  for h in range(heads)]
        upd = [_dot_tn(v[:, hv[h]], k_s[:, hk[h]]) for h in range(heads)]
        if factorable:
            o = [o[h] + _dot(sc[h], v[:, hv[h]]) for h in range(heads)]
        dec = dec_ref[c]
        for h in range(heads):
            o_ref[rows, hv[h]] = o[h]
            st_ref[h] = s_t[h] * dec[0:1, hk[h]] + upd[h]
        if factorable:
            return carry

        b = b_ref[rows, :]
        k = k_ref[rows, :]
        s_idx = lax.broadcasted_iota(jnp.int32, (chunk, 1), 0)
        sub = lax.broadcasted_iota(jnp.int32, (SUBLANES, 1), 0)

        def token_group(t8, carry):
            rows8 = pl.ds(pl.multiple_of(c * chunk + t8 * SUBLANES, SUBLANES), SUBLANES)
            q8 = q_ref[rows8, :]
            b8 = b_ref[rows8, :]
            out = [jnp.zeros((SUBLANES, dv), F32) for _ in range(heads)]
            for r in range(SUBLANES):
                t = t8 * SUBLANES + r
                decay = jnp.exp(jnp.where(s_idx <= t, b8[r:r + 1, :] - b, -jnp.inf))
                w = q8[r:r + 1, :] * decay * k
                for h in range(heads):
                    sc = jnp.sum(w[:, h * dk:(h + 1) * dk], axis=1, keepdims=True)
                    o_t = jnp.sum(sc * v_ref[rows, h * dv:(h + 1) * dv], axis=0, keepdims=True)
                    out[h] = jnp.where(sub == r, o_t, out[h])
            for h in range(heads):
                o_ref[rows8, h * dv:(h + 1) * dv] += out[h]
            return carry

        return lax.fori_loop(0, chunk // SUBLANES, token_group, carry)

    for flag, pred in ((True, factorable), (False, jnp.logical_not(factorable))):
        @pl.when(pred)
        def _():
            if flag:
                kt_ref[...] = (k3 * jnp.exp(-b3)).astype(BF16).reshape(tile, kd)
            lax.fori_loop(0, n_chunks, functools.partial(chunk_step, factorable=flag), 0)

    gn = gn_ref[...]
    r = r_ref[...]
    parts = []
    for h in range(heads):
        vs = slice(h * dv, (h + 1) * dv)
        parts.append(_rms(o_ref[:, vs], gn[:, vs]))
    on = jnp.concatenate(parts, axis=1)
    on = (on * (r * jax.nn.sigmoid(r))).astype(BF16)
    y_ref[...] = x_ref[...] + _dot(on, wo_ref[...])

    @pl.when(t == pl.num_programs(1) - 1)
    def _():
        for h in range(heads):
            sfin_ref[0, h] = st_ref[h].T


def _gla_mix(x, q, k, v, r, g, s0, gn, wo, *, batch, chunk, tile):
    n, d = x.shape
    seq = n // batch
    heads, dk, dv = s0.shape[1:]
    nt = seq // tile
    tri = jnp.tril(jnp.ones((chunk, chunk), F32)).astype(BF16)
    row = lambda w: pl.BlockSpec((tile, w), lambda b, t: (b * nt + t, 0))
    y, sfin = pl.pallas_call(
        functools.partial(_gla_mix_body, heads=heads, chunk=chunk),
        grid=(batch, nt),
        in_specs=[row(d), row(heads * dk), row(heads * dk), row(heads * dv), row(heads * dv),
                  row(heads * dk),
                  pl.BlockSpec((1, heads, dk, dv), lambda b, t: (b, 0, 0, 0)),
                  _const_spec(gn), _const_spec(wo), _const_spec(tri)],
        out_specs=[row(d), pl.BlockSpec((1, heads, dk, dv), lambda b, t: (b, 0, 0, 0))],
        out_shape=[jax.ShapeDtypeStruct((n, d), F32),
                   jax.ShapeDtypeStruct((batch, heads, dk, dv), F32)],
        scratch_shapes=[pltpu.VMEM((heads, dv, dk), F32), pltpu.VMEM((tile, heads * dv), F32),
                        pltpu.VMEM((tile, heads * dk), F32)]
                       + [pltpu.VMEM((tile, heads * dk), BF16)] * 3
                       + [pltpu.VMEM((tile // chunk, SUBLANES, heads * dk), F32)],
        compiler_params=_params("arbitrary", "arbitrary"),
    )(x, q, k, v, r, g, s0, gn, wo, tri)
    return y, sfin


def _sb_prompt_body(bias_ref, q_ref, k_ref, v_ref, tri_ref, o_ref,
                    lb_ref, hi_ref, zm_ref, qh_ref, kh_ref, acc_ref, run_ref, *, blk, hd):
    h2 = pl.program_id(1)
    nq = k_ref.shape[1]
    tri = tri_ref[...]
    row = lax.broadcasted_iota(jnp.int32, (blk, blk), 0)
    col = lax.broadcasted_iota(jnp.int32, (blk, blk), 1)
    heads = range(LANES // hd)

    zm_ref[...] = jnp.where(col < row, 0.0, -jnp.inf)
    q_all = q_ref[0].astype(F32)
    k_all = k_ref[0].astype(F32)
    q_lane = lax.broadcasted_iota(jnp.int32, q_all.shape, 1)
    k_row = lax.broadcasted_iota(jnp.int32, k_all.shape, 1)
    for i in heads:
        b = jnp.full(q_all.shape, bias_ref[h2 * len(heads) + i], F32)
        b_hi = b.astype(BF16).astype(F32)
        spare = lambda lane: (lane + (LANES - i * hd)) % LANES - hd
        own = lambda lane: (lane >= i * hd) & (lane < (i + 1) * hd)
        q_spare = jnp.where(spare(q_lane) == 0, b_hi, jnp.where(spare(q_lane) == 1, b - b_hi, 0.0))
        qh_ref[i] = jnp.where(own(q_lane), q_all, q_spare).astype(BF16)
        k_spare = jnp.where(spare(k_row) < 2, 1.0, 0.0)
        kh_ref[i] = jnp.where(own(k_row), k_all, k_spare).astype(BF16)

    def step(front, back, diag):
        if back is not None:
            qb, kb = back
            vt = v_ref[0, kb]
            hi = [hi_ref[i] for i in heads]
            later = [_dot(hi[i], tri) for i in heads]
        if front is not None:
            qf, kf = front
            q_rows = pl.ds(pl.multiple_of(qf * blk, blk), blk)

        def logits(i):
            z = _dot(qh_ref[i, q_rows, :], kh_ref[i, kf])
            return z + zm_ref[...] if diag else z

        def finish(i):
            total = later[i][:, 0:1] + hi[i][:, 0:1].astype(F32)
            if diag:
                a = jnp.exp(lb_ref[i] - later[i]).astype(BF16)
                run_ref[i, qb] = -total
                acc_ref[i, qb] = _dot_nt(a, vt)
            else:
                run = run_ref[i, qb]
                a = jnp.exp(lb_ref[i] - later[i] + run).astype(BF16)
                run_ref[i, qb] = run - total
                acc_ref[i, qb] += _dot_nt(a, vt)

        def soft(i, z):
            s = _softplus(z)
            lb_ref[i] = z - s
            hi_ref[i] = s.astype(BF16)

        z = logits(0) if front is not None else None
        for i in heads:
            if back is not None:
                finish(i)
            if front is not None:
                if i > 0:
                    z = logits(i)
                soft(i, z)

    def sweep(first, n, following, diag):
        step(first, None, diag)

        def unrolled(_, pair):
            for _ in range(SB_UNROLL):
                nxt = following(pair)
                step(nxt, pair, diag)
                pair = nxt
            return pair

        pair = lax.fori_loop(0, (n - 1) // SB_UNROLL, unrolled, first)
        for _ in range((n - 1) % SB_UNROLL):
            nxt = following(pair)
            step(nxt, pair, diag)
            pair = nxt
        step(None, pair, diag)

    zero, one = jnp.int32(0), jnp.int32(1)
    sweep((zero, zero), nq, lambda p: (p[0] + 1, p[1] + 1), True)
    if nq > 1:
        older = lambda p: (jnp.where(p[1] > 0, p[0], p[0] + 1), jnp.where(p[1] > 0, p[1] - 1, p[0]))
        sweep((one, zero), nq * (nq - 1) // 2, older, False)

    lane = lax.broadcasted_iota(jnp.int32, (blk, LANES), 1)
    for qi in range(nq):
        o = acc_ref[0, qi]
        for i in heads[1:]:
            o = jnp.where((lane >= i * hd) & (lane < (i + 1) * hd), acc_ref[i, qi], o)
        o_ref[0, qi * blk:(qi + 1) * blk, :] = o.astype(o_ref.dtype)


def _sb_prompt(q, kt, vt, bias, *, batch, hd):
    n, d = q.shape
    seq = n // batch
    nq, blk = kt.shape[1], kt.shape[3]
    q3 = q.reshape(batch, seq, d)
    tri = (jnp.arange(blk)[:, None] > jnp.arange(blk)[None, :]).astype(BF16)
    heads = LANES // hd
    assert heads >= 2, "the logit offset rides in lanes of the tile's other heads"
    q_spec = pl.BlockSpec((1, seq, LANES), lambda b, h: (b, 0, h))
    kv_spec = pl.BlockSpec((1, nq, LANES, blk), lambda b, h: (b, 0, h, 0))
    stage = lambda dt: pltpu.VMEM((heads, blk, blk), dt)
    o = pl.pallas_call(
        functools.partial(_sb_prompt_body, blk=blk, hd=hd),
        grid=(batch, d // LANES),
        in_specs=[pl.BlockSpec(memory_space=pltpu.SMEM), q_spec, kv_spec, kv_spec, _const_spec(tri)],
        out_specs=q_spec,
        out_shape=jax.ShapeDtypeStruct((batch, seq, d), BF16),
        scratch_shapes=[stage(F32), stage(BF16),
                        pltpu.VMEM((blk, blk), F32),
                        pltpu.VMEM((heads, seq, LANES), BF16),
                        pltpu.VMEM((heads, nq, LANES, blk), BF16),
                        pltpu.VMEM((heads, nq, blk, LANES), F32),
                        pltpu.VMEM((heads, nq, blk, 1), F32)],
        compiler_params=_params("parallel", "parallel"),
    )(bias, q3, kt, vt, tri)
    return o.reshape(n, d)


def _sb_sample_body(pt_ref, qbd_ref, bias_ref, kn_ref, vn_ref, tri_ref, *rest, pages, heads, hd):
    k_refs = rest[:pages]
    v_refs = rest[pages:2 * pages]
    o_ref, acc_ref, run_ref, kpad_ref, vpad_ref = rest[2 * pages:]
    p = pl.program_id(1)
    t_new = kn_ref.shape[1]
    rows = qbd_ref.shape[1]
    psz = kpad_ref.shape[0]
    qbd = qbd_ref[0]
    bias = bias_ref[...]
    tri = tri_ref[...]

    def sweep(kps, vps, causal):
        z = [(_dot(qbd, kp) if causal is None else _dot_nt(qbd, kp)) + bias for kp in kps]
        sp = [_softplus(zi) for zi in z]
        lb = [zi - si for zi, si in zip(z, sp)]
        if causal is not None:
            sp = [jnp.where(causal, si, 0.0) for si in sp]
            lb = [jnp.where(causal, li, -jnp.inf) for li in lb]
        both = [_dot(jnp.concatenate(_split_bf16(si), axis=1), tri) for si in sp]
        run = run_ref[...]
        acc = acc_ref[...]
        for li, bi, vp in zip(lb, both, vps):
            a = jnp.exp(li - bi[:, :psz] + run).astype(BF16)
            run = run - bi[:, psz:]
            acc = acc + (_dot_nt(a, vp) if causal is None else _dot(a, vp))
        run_ref[...] = run
        acc_ref[...] = acc

    @pl.when(p == 0)
    def _():
        acc_ref[...] = jnp.zeros_like(acc_ref)
        run_ref[...] = jnp.zeros_like(run_ref)
        kpad_ref[...] = jnp.zeros_like(kpad_ref)
        vpad_ref[...] = jnp.zeros_like(vpad_ref)
        kpad_ref[0:t_new, :] = kn_ref[0]
        vpad_ref[0:t_new, :] = vn_ref[0]
        tq = lax.broadcasted_iota(jnp.int32, (rows, psz), 0) % t_new
        causal = lax.broadcasted_iota(jnp.int32, (rows, psz), 1) < tq
        sweep([kpad_ref[...].astype(BF16)], [vpad_ref[...].astype(BF16)], causal)

    sweep([r[0].astype(BF16) for r in k_refs], [r[0].astype(BF16) for r in v_refs], None)

    @pl.when(p == pl.num_programs(1) - 1)
    def _():
        lane = lax.broadcasted_iota(jnp.int32, (t_new, LANES), 1)
        per_tile = LANES // hd
        for gidx in range(heads // per_tile):
            cols = slice(gidx * LANES, (gidx + 1) * LANES)
            o = acc_ref[gidx * per_tile * t_new:(gidx * per_tile + 1) * t_new, cols]
            for i in range(1, per_tile):
                h = gidx * per_tile + i
                o = jnp.where(lane >= i * hd, acc_ref[h * t_new:(h + 1) * t_new, cols], o)
            o_ref[0, :, cols] = o


def _sb_sample(q, k_new, v_new, bias, cache_k, cache_v, layer, page_table, *, heads, hd):
    db, t_new, d = q.shape
    n_layers, n_phys, psz = cache_k.shape[:3]
    n_pages = page_table.shape[1]
    pages = PAGES_PER_STEP if n_pages % PAGES_PER_STEP == 0 else 1
    ck = cache_k.transpose(0, 1, 3, 4, 2).reshape(n_layers * n_phys, d, psz)
    cv = cache_v.transpose(0, 1, 3, 4, 2).reshape(n_layers * n_phys, d, psz)
    head_of_lane = jnp.arange(d) // hd
    qbd = jnp.where(head_of_lane[None, None, None, :] == jnp.arange(heads)[None, :, None, None],
                    q[:, None, :, :], jnp.zeros((), q.dtype)).reshape(db, heads * t_new, d)
    bias_rows = jnp.broadcast_to(jnp.repeat(bias.astype(F32), t_new)[:, None], (heads * t_new, psz))
    tri = jnp.concatenate([(jnp.arange(psz)[:, None] > jnp.arange(psz)[None, :]).astype(BF16),
                           jnp.ones((psz, psz), BF16)], axis=1)
    tri = jnp.tile(tri, (2, 1))
    base = layer * n_phys

    def page_spec(i):
        return pl.BlockSpec(
            (1, d, psz), lambda b, p, pt: (pt[b, n_pages - 1 - (p * pages + i)] + base, 0, 0))

    per_seq = lambda shape: pl.BlockSpec(shape, lambda b, p, pt: (b, 0, 0))
    const = lambda a: pl.BlockSpec(a.shape, lambda b, p, pt: (0,) * a.ndim)
    grid_spec = pltpu.PrefetchScalarGridSpec(
        num_scalar_prefetch=1,
        grid=(db, n_pages // pages),
        in_specs=[per_seq((1, heads * t_new, d)), const(bias_rows),
                  per_seq((1, t_new, d)), per_seq((1, t_new, d)), const(tri)]
                 + [page_spec(i) for i in range(pages)] * 2,
        out_specs=per_seq((1, t_new, d)),
        scratch_shapes=[pltpu.VMEM((heads * t_new, d), F32), pltpu.VMEM((heads * t_new, psz), F32),
                        pltpu.VMEM((psz, d), F32), pltpu.VMEM((psz, d), F32)],
    )
    return pl.pallas_call(
        functools.partial(_sb_sample_body, pages=pages, heads=heads, hd=hd),
        grid_spec=grid_spec,
        out_shape=jax.ShapeDtypeStruct((db, t_new, d), F32),
        compiler_params=_params("parallel", "arbitrary"),
    )(page_table, qbd, bias_rows, k_new, v_new, tri, *([ck] * pages), *([cv] * pages))


def kernel(x_prompt, x_sample, state_gla, cache_k, cache_v, page_table, p_prompt, p_sample,
           norm_ffn1, ffn1_w_in, ffn1_w_out, norm_mix, gla_w_in, gla_w_a2, gla_b_a, gla_norm,
           gla_w_out, sb_w_qkv, sb_bias, sb_w_out, norm_ffn2, ffn2_w_in, ffn2_w_out, norm_pe,
           pe_w_gate, pe_w_proj, norm_final):
    bp, tp, d = x_prompt.shape
    bs, ts, _ = x_sample.shape
    depth = norm_ffn1.shape[0]
    n_mixers = 2
    gla_heads, gla_dk, gla_dv = state_gla.shape[2:]
    dk_total, dv_total = gla_heads * gla_dk, gla_heads * gla_dv
    rank = gla_w_a2.shape[1]
    sb_heads, sb_hd = cache_k.shape[3:]
    row = lambda a: a.reshape(1, -1)

    groups = [dict(x=x_prompt.reshape(bp * tp, d), p=p_prompt.reshape(depth * bp * tp, -1), batch=bp),
              dict(x=x_sample.reshape(bs * ts, d), p=p_sample.reshape(depth * bs * ts, -1), batch=bs)]
    gla_states = [[], []]
    new_k = [[], []]
    new_v = [[], []]

    for i in range(depth):
        j = i // n_mixers
        w1_in, w1_out = ffn1_w_in[i].astype(BF16), ffn1_w_out[i].astype(BF16)
        w2_in, w2_out = ffn2_w_in[i].astype(BF16), ffn2_w_out[i].astype(BF16)
        wg, wp = pe_w_gate[i].astype(BF16), pe_w_proj[i].astype(BF16)
        if i % n_mixers == 0:
            w_main = gla_w_in[j, :, :2 * dk_total + 2 * dv_total].astype(BF16)
            w_a = jnp.pad(gla_w_in[j, :, 2 * dk_total + 2 * dv_total:],
                          ((0, 0), (0, LANES - rank))).astype(BF16)
            w_a2 = jnp.pad(gla_w_a2[j], ((0, LANES - rank), (0, 0))).astype(BF16)
            w_o = gla_w_out[j].astype(BF16)
        else:
            w_qkv = sb_w_qkv[j].astype(BF16)
            w_o = sb_w_out[j].astype(BF16)

        for gi, grp in enumerate(groups):
            x = grp["x"]
            is_prompt = gi == 0
            (x,) = _row_call(_ffn_body, [x], [row(norm_ffn1[i]), w1_in, w1_out], [(d, F32)], 512)
            if i % n_mixers == 0:
                q, k, v, r, g = _row_call(
                    _gla_proj_body, [x], [row(norm_mix[i]), w_main, w_a, w_a2, row(gla_b_a[j])],
                    [(dk_total, F32), (dk_total, F32), (dv_total, F32), (dv_total, F32), (dk_total, F32)],
                    512, dk_total=dk_total, dv_total=dv_total, q_scale=gla_dk ** -0.5)
                if is_prompt:
                    s0 = jnp.zeros((bp,) + state_gla.shape[2:], F32)
                    x, s_fin = _gla_mix(x, q, k, v, r, g, s0, row(gla_norm[j]), w_o,
                                        batch=bp, chunk=GLA_CHUNK, tile=min(512, tp))
                else:
                    x, s_fin = _gla_mix(x, q, k, v, r, g, state_gla[j].astype(F32), row(gla_norm[j]),
                                        w_o, batch=bs, chunk=ts, tile=ts)
                gla_states[gi].append(s_fin.astype(state_gla.dtype))
            else:
                if is_prompt:
                    q, kt, vt, ktb, vtb = _sb_proj_t(
                        x, row(norm_mix[i]), w_qkv[:, :d], w_qkv[:, d:2 * d].T, w_qkv[:, 2 * d:].T,
                        batch=bp, tm=512, blk=min(SB_BLOCK, tp), q_scale=sb_hd ** -0.5)
                    o = _sb_prompt(q, ktb, vtb, sb_bias[j].astype(F32), batch=bp, hd=sb_hd)
                    k, v = (a.reshape(bp, sb_heads, sb_hd, tp).transpose(0, 3, 1, 2) for a in (kt, vt))
                else:
                    q, k, v = _row_call(
                        _sb_proj_body, [x], [row(norm_mix[i]), w_qkv],
                        [(d, BF16), (d, F32), (d, F32)], 512, d=d, q_scale=sb_hd ** -0.5)
                    o = _sb_sample(q.reshape(bs, ts, d), k.reshape(bs, ts, d), v.reshape(bs, ts, d),
                                   sb_bias[j], cache_k, cache_v, j, page_table,
                                   heads=sb_heads, hd=sb_hd).reshape(bs * ts, d)
                    k, v = (a.reshape(bs, ts, sb_heads, sb_hd) for a in (k, v))
                new_k[gi].append(k.astype(cache_k.dtype))
                new_v[gi].append(v.astype(cache_v.dtype))
            has_mix = i % n_mixers != 0
            (x,) = _row_call(
                _post_mixer_body, [x] + ([o] if has_mix else []) + [(grp["p"], i * x.shape[0])],
                ([w_o] if has_mix else []) + [row(norm_ffn2[i]), w2_in, w2_out, row(norm_pe[i]), wg, wp,
                                              row(norm_final)],
                [(d, F32)], 512, has_mix=has_mix, final=(i == depth - 1))
            grp["x"] = x

    return (groups[0]["x"].reshape(bp, tp, d), groups[1]["x"].reshape(bs, ts, d),
            jnp.stack(gla_states[0]), jnp.stack(gla_states[1]),
            jnp.stack(new_k[0]), jnp.stack(new_v[0]), jnp.stack(new_k[1]), jnp.stack(new_v[1]))
```

```python
import functools

import jax
import jax.numpy as jnp
from jax import lax
from jax.experimental import pallas as pl
from jax.experimental.pallas import tpu as pltpu

F32 = jnp.float32
BF16 = jnp.bfloat16

NORM_EPS = 1e-6
LOG2E = 1.4426950408889634
GLA_TAU = 16.0
GLA_CHUNK = 128
SB_BLOCK = 256
SB_UNROLL = 17
PAGES_PER_STEP = 16
GLA_MAX_CHUNK_DECAY = 80.0
LANES = 128
SUBLANES = 8
VMEM_LIMIT = 56 * 1024 * 1024


def _params(*sem):
    return pltpu.CompilerParams(dimension_semantics=sem, vmem_limit_bytes=VMEM_LIMIT)


def _rms(x, g):
    return x * lax.rsqrt(jnp.mean(x * x, axis=-1, keepdims=True) + NORM_EPS) * g


def _dot(a, b):
    return jnp.dot(a, b, preferred_element_type=F32)


def _dot_nt(a, b):
    return lax.dot_general(a, b, (((1,), (1,)), ((), ())), preferred_element_type=F32)


def _dot_tn(a, b):
    return lax.dot_general(a, b, (((0,), (0,)), ((), ())), preferred_element_type=F32)


def _softplus(z):
    neg_abs = lax.bitcast_convert_type(
        lax.bitcast_convert_type(z, jnp.uint32) | jnp.uint32(0x80000000), F32)
    return jnp.maximum(z, 0.0) + jnp.log(1.0 + jnp.exp(neg_abs))


def _split_bf16(x):
    hi = x.astype(BF16)
    lo = (x - hi.astype(F32)).astype(BF16)
    return hi, lo


def _const_spec(a):
    nd = a.ndim
    return pl.BlockSpec(a.shape, lambda *_: (0,) * nd, pipeline_mode=pl.Buffered(1))


def _row_call(body, row_args, const_args, out_dims, tm, **kw):
    n = row_args[0].shape[0]
    tm = min(tm, n)
    assert n % tm == 0
    row_args = [a if isinstance(a, tuple) else (a, 0) for a in row_args]
    assert all(first % tm == 0 for _, first in row_args)
    in_specs = [pl.BlockSpec((tm, a.shape[1]), lambda i, off=first // tm: (i + off, 0))
                for a, first in row_args]
    row_args = [a for a, _ in row_args]
    in_specs += [_const_spec(a) for a in const_args]
    out_specs = [pl.BlockSpec((tm, d), lambda i: (i, 0)) for d, _ in out_dims]
    out_shape = [jax.ShapeDtypeStruct((n, d), dt) for d, dt in out_dims]
    outs = pl.pallas_call(
        functools.partial(body, **kw),
        grid=(n // tm,),
        in_specs=in_specs,
        out_specs=out_specs,
        out_shape=out_shape,
        compiler_params=_params("parallel"),
    )(*row_args, *const_args)
    return outs


def _ffn(x, g_ref, win_ref, wout_ref):
    xn = _rms(x, g_ref[...]).astype(BF16)
    h = _dot(xn, win_ref[...])
    dff = wout_ref.shape[0]
    gate, up = h[:, :dff], h[:, dff:]
    act = (gate * jax.nn.sigmoid(gate) * up).astype(BF16)
    return x + 0.5 * _dot(act, wout_ref[...])


def _ffn_body(x_ref, g_ref, win_ref, wout_ref, o_ref):
    o_ref[...] = _ffn(x_ref[...], g_ref, win_ref, wout_ref)


def _post_mixer_body(*refs, has_mix, final):
    x_ref, *refs = refs
    x = x_ref[...]
    if has_mix:
        mix_ref, *refs = refs
    p_ref, *refs = refs
    if has_mix:
        wmix_ref, *refs = refs
        x = x + _dot(mix_ref[...].astype(BF16), wmix_ref[...])
    g_ref, win_ref, wout_ref, gpe_ref, wg_ref, wp_ref, gf_ref, o_ref = refs
    x = _ffn(x, g_ref, win_ref, wout_ref)
    xn = _rms(x, gpe_ref[...]).astype(BF16)
    gate = jax.nn.sigmoid(_dot(xn, wg_ref[...]))
    x = x + gate * _dot(p_ref[...].astype(BF16), wp_ref[...])
    if final:
        x = _rms(x, gf_ref[...])
    o_ref[...] = x


def _gla_proj_body(x_ref, g_ref, w_ref, wa_ref, wa2_ref, ba_ref,
                   q_ref, k_ref, v_ref, r_ref, gate_ref, *, dk_total, dv_total, q_scale):
    xn = _rms(x_ref[...], g_ref[...]).astype(BF16)
    y = _dot(xn, w_ref[...])
    q_ref[...] = y[:, :dk_total] * q_scale
    k_ref[...] = y[:, dk_total:2 * dk_total]
    v_ref[...] = y[:, 2 * dk_total:2 * dk_total + dv_total].astype(v_ref.dtype)
    r_ref[...] = y[:, 2 * dk_total + dv_total:]
    a = _dot(xn, wa_ref[...]).astype(BF16)
    ga = _dot(a, wa2_ref[...]) + ba_ref[...]
    gate_ref[...] = (jnp.minimum(ga, 0.0) - jnp.log1p(jnp.exp(-jnp.abs(ga)))) * (1.0 / GLA_TAU)


def _sb_proj_body(x_ref, g_ref, w_ref, q_ref, k_ref, v_ref, *, d, q_scale):
    xn = _rms(x_ref[...], g_ref[...]).astype(BF16)
    y = _dot(xn, w_ref[...])
    q_ref[...] = (y[:, :d] * q_scale).astype(BF16)
    k_ref[...] = y[:, d:2 * d]
    v_ref[...] = y[:, 2 * d:]


def _sb_proj_t_body(x_ref, g_ref, wq_ref, wkt_ref, wvt_ref, q_ref, kt_ref, vt_ref, ktb_ref, vtb_ref,
                    *, q_scale):
    xn = _rms(x_ref[...], g_ref[...]).astype(BF16)
    q_ref[...] = (_dot(xn, wq_ref[...]) * q_scale).astype(BF16)
    blk = ktb_ref.shape[3]
    for w_ref, t_ref, tb_ref in ((wkt_ref, kt_ref, ktb_ref), (wvt_ref, vt_ref, vtb_ref)):
        yt = _dot_nt(w_ref[...], xn)
        t_ref[0] = yt
        for j in range(tb_ref.shape[1]):
            tb_ref[0, j] = yt[:, j * blk:(j + 1) * blk].astype(BF16)


def _sb_proj_t(x, g, wq, wkt, wvt, *, batch, tm, blk, q_scale):
    n, d = x.shape
    seq = n // batch
    tm = min(tm, seq)
    nt = seq // tm
    row = pl.BlockSpec((tm, d), lambda b, t: (b * nt + t, 0))
    tr = pl.BlockSpec((1, d, tm), lambda b, t: (b, 0, t))
    trb = pl.BlockSpec((1, tm // blk, d, blk), lambda b, t: (b, t, 0, 0))
    return pl.pallas_call(
        functools.partial(_sb_proj_t_body, q_scale=q_scale),
        grid=(batch, nt),
        in_specs=[row, _const_spec(g), _const_spec(wq), _const_spec(wkt), _const_spec(wvt)],
        out_specs=[row, tr, tr, trb, trb],
        out_shape=[jax.ShapeDtypeStruct((n, d), BF16),
                   jax.ShapeDtypeStruct((batch, d, seq), F32),
                   jax.ShapeDtypeStruct((batch, d, seq), F32),
                   jax.ShapeDtypeStruct((batch, seq // blk, d, blk), BF16),
                   jax.ShapeDtypeStruct((batch, seq // blk, d, blk), BF16)],
        compiler_params=_params("parallel", "parallel"),
    )(x, g, wq, wkt, wvt)


def _gla_mix_body(x_ref, q_ref, k_ref, v_ref, r_ref, g_ref, s0_ref, gn_ref, wo_ref, tri_ref,
                  y_ref, sfin_ref, st_ref, o_ref, b_ref, qt_ref, kt_ref, ks_ref, dec_ref,
                  *, heads, chunk):
    t = pl.program_id(1)
    tile = x_ref.shape[0]
    dk = q_ref.shape[1] // heads
    dv = v_ref.shape[1] // heads

    @pl.when(t == 0)
    def _():
        for h in range(heads):
            st_ref[h] = s0_ref[0, h].T

    tri = tri_ref[...]
    causal = (lax.broadcasted_iota(jnp.int32, (chunk, chunk), 1)
              <= lax.broadcasted_iota(jnp.int32, (chunk, chunk), 0))

    n_chunks = tile // chunk
    kd = heads * dk

    for c in range(n_chunks):
        rows = slice(c * chunk, (c + 1) * chunk)
        g_hi, g_lo = _split_bf16(g_ref[rows, :])
        b_ref[rows, :] = _dot(tri, g_hi) + _dot(tri, g_lo)
    b3 = b_ref[...].reshape(n_chunks, chunk, kd)
    b_last = b3[:, chunk - 1:chunk, :]
    q3 = q_ref[...].reshape(n_chunks, chunk, kd)
    k3 = k_ref[...].reshape(n_chunks, chunk, kd)
    qt_ref[...] = (q3 * jnp.exp(b3)).astype(BF16).reshape(tile, kd)
    ks_ref[...] = (k3 * jnp.exp(b_last - b3)).astype(BF16).reshape(tile, kd)
    dec_ref[...] = jnp.broadcast_to(jnp.exp(b_last), (n_chunks, SUBLANES, kd))
    factorable = jnp.min(b_last) >= -GLA_MAX_CHUNK_DECAY

    def chunk_step(c, carry, *, factorable):
        rows = pl.ds(pl.multiple_of(c * chunk, chunk), chunk)
        q_t = qt_ref[rows, :]
        k_s = ks_ref[rows, :]
        v = v_ref[rows, :].astype(BF16)
        s_t = [st_ref[h] for h in range(heads)]
        hk = [slice(h * dk, (h + 1) * dk) for h in range(heads)]
        hv = [slice(h * dv, (h + 1) * dv) for h in range(heads)]
        o = [_dot_nt(q_t[:, hk[h]], s_t[h].astype(BF16)) for h in range(heads)]
        if factorable:
            k_t = kt_ref[rows, :]
            sc = [jnp.where(causal, _dot_nt(q_t[:, hk[h]], k_t[:, hk[h]]), 0.0).astype(BF16)
                  for h in range(heads)]
        upd = [_dot_tn(v[:, hv[h]], k_s[:, hk[h]]) for h in range(heads)]
        if factorable:
            o = [o[h] + _dot(sc[h], v[:, hv[h]]) for h in range(heads)]
        dec = dec_ref[c]
        for h in range(heads):
            o_ref[rows, hv[h]] = o[h]
            st_ref[h] = s_t[h] * dec[0:1, hk[h]] + upd[h]
        if factorable:
            return carry

        b = b_ref[rows, :]
        k = k_ref[rows, :]
        s_idx = lax.broadcasted_iota(jnp.int32, (chunk, 1), 0)
        sub = lax.broadcasted_iota(jnp.int32, (SUBLANES, 1), 0)

        def token_group(t8, carry):
            rows8 = pl.ds(pl.multiple_of(c * chunk + t8 * SUBLANES, SUBLANES), SUBLANES)
            q8 = q_ref[rows8, :]
            b8 = b_ref[rows8, :]
            out = [jnp.zeros((SUBLANES, dv), F32) for _ in range(heads)]
            for r in range(SUBLANES):
                t = t8 * SUBLANES + r
                decay = jnp.exp(jnp.where(s_idx <= t, b8[r:r + 1, :] - b, -jnp.inf))
                w = q8[r:r + 1, :] * decay * k
                for h in range(heads):
                    sc = jnp.sum(w[:, h * dk:(h + 1) * dk], axis=1, keepdims=True)
                    v_h = v_ref[rows, h * dv:(h + 1) * dv].astype(F32)
                    o_t = jnp.sum(sc * v_h, axis=0, keepdims=True)
                    out[h] = jnp.where(sub == r, o_t, out[h])
            for h in range(heads):
                o_ref[rows8, h * dv:(h + 1) * dv] += out[h]
            return carry

        return lax.fori_loop(0, chunk // SUBLANES, token_group, carry)

    for flag, pred in ((True, factorable), (False, jnp.logical_not(factorable))):
        @pl.when(pred)
        def _():
            if flag:
                kt_ref[...] = (k3 * jnp.exp(-b3)).astype(BF16).reshape(tile, kd)
            lax.fori_loop(0, n_chunks, functools.partial(chunk_step, factorable=flag), 0)

    gn = gn_ref[...]
    r = r_ref[...]
    parts = []
    for h in range(heads):
        vs = slice(h * dv, (h + 1) * dv)
        parts.append(_rms(o_ref[:, vs], gn[:, vs]))
    on = jnp.concatenate(parts, axis=1)
    on = (on * (r * jax.nn.sigmoid(r))).astype(BF16)
    y_ref[...] = x_ref[...] + _dot(on, wo_ref[...])

    @pl.when(t == pl.num_programs(1) - 1)
    def _():
        for h in range(heads):
            sfin_ref[0, h] = st_ref[h].T


def _gla_mix(x, q, k, v, r, g, s0, gn, wo, *, batch, chunk, tile):
    n, d = x.shape
    seq = n // batch
    heads, dk, dv = s0.shape[1:]
    nt = seq // tile
    tri = jnp.tril(jnp.ones((chunk, chunk), F32)).astype(BF16)
    row = lambda w: pl.BlockSpec((tile, w), lambda b, t: (b * nt + t, 0))
    y, sfin = pl.pallas_call(
        functools.partial(_gla_mix_body, heads=heads, chunk=chunk),
        grid=(batch, nt),
        in_specs=[row(d), row(heads * dk), row(heads * dk), row(heads * dv), row(heads * dv),
                  row(heads * dk),
                  pl.BlockSpec((1, heads, dk, dv), lambda b, t: (b, 0, 0, 0)),
                  _const_spec(gn), _const_spec(wo), _const_spec(tri)],
        out_specs=[row(d), pl.BlockSpec((1, heads, dk, dv), lambda b, t: (b, 0, 0, 0))],
        out_shape=[jax.ShapeDtypeStruct((n, d), F32),
                   jax.ShapeDtypeStruct((batch, heads, dk, dv), F32)],
        scratch_shapes=[pltpu.VMEM((heads, dv, dk), F32), pltpu.VMEM((tile, heads * dv), F32),
                        pltpu.VMEM((tile, heads * dk), F32)]
                       + [pltpu.VMEM((tile, heads * dk), BF16)] * 3
                       + [pltpu.VMEM((tile // chunk, SUBLANES, heads * dk), F32)],
        compiler_params=_params("arbitrary", "arbitrary"),
    )(x, q, k, v, r, g, s0, gn, wo, tri)
    return y, sfin


def _sb_prompt_body(bias_ref, q_ref, k_ref, v_ref, tri_ref, o_ref,
                    lb_ref, hi_ref, zm_ref, qh_ref, kh_ref, acc_ref, run_ref, *, blk, hd):
    h2 = pl.program_id(1)
    nq = k_ref.shape[1]
    tri = tri_ref[...]
    row = lax.broadcasted_iota(jnp.int32, (blk, blk), 0)
    col = lax.broadcasted_iota(jnp.int32, (blk, blk), 1)
    heads = range(LANES // hd)

    zm_ref[...] = jnp.where(col < row, 0.0, -jnp.inf)
    q_all = q_ref[0].astype(F32)
    k_all = k_ref[0].astype(F32)
    q_lane = lax.broadcasted_iota(jnp.int32, q_all.shape, 1)
    k_row = lax.broadcasted_iota(jnp.int32, k_all.shape, 1)
    for i in heads:
        b = jnp.full(q_all.shape, bias_ref[h2 * len(heads) + i], F32)
        b_hi = b.astype(BF16).astype(F32)
        spare = lambda lane: (lane + (LANES - i * hd)) % LANES - hd
        own = lambda lane: (lane >= i * hd) & (lane < (i + 1) * hd)
        q_spare = jnp.where(spare(q_lane) == 0, b_hi, jnp.where(spare(q_lane) == 1, b - b_hi, 0.0))
        qh_ref[i] = jnp.where(own(q_lane), q_all, q_spare).astype(BF16)
        k_spare = jnp.where(spare(k_row) < 2, 1.0, 0.0)
        kh_ref[i] = jnp.where(own(k_row), k_all, k_spare).astype(BF16)

    def step(front, back, diag):
        if back is not None:
            qb, kb = back
            vt = v_ref[0, kb]
            hi = [hi_ref[i] for i in heads]
            later = [_dot(hi[i], tri) for i in heads]
        if front is not None:
            qf, kf = front
            q_rows = pl.ds(pl.multiple_of(qf * blk, blk), blk)

        def logits(i):
            z = _dot(qh_ref[i, q_rows, :], kh_ref[i, kf])
            return z + zm_ref[...] if diag else z

        def finish(i):
            total = later[i][:, 0:1] + hi[i][:, 0:1].astype(F32)
            if diag:
                a = jnp.exp(lb_ref[i] - later[i]).astype(BF16)
                run_ref[i, qb] = -total
                acc_ref[i, qb] = _dot_nt(a, vt)
            else:
                run = run_ref[i, qb]
                a = jnp.exp(lb_ref[i] - later[i] + run).astype(BF16)
                run_ref[i, qb] = run - total
                acc_ref[i, qb] += _dot_nt(a, vt)

        def soft(i, z):
            s = _softplus(z)
            lb_ref[i] = z - s
            hi_ref[i] = s.astype(BF16)

        z = logits(0) if front is not None else None
        for i in heads:
            if back is not None:
                finish(i)
            if front is not None:
                if i > 0:
                    z = logits(i)
                soft(i, z)

    def sweep(first, n, following, diag):
        step(first, None, diag)

        def unrolled(_, pair):
            for _ in range(SB_UNROLL):
                nxt = following(pair)
                step(nxt, pair, diag)
                pair = nxt
            return pair

        pair = lax.fori_loop(0, (n - 1) // SB_UNROLL, unrolled, first)
        for _ in range((n - 1) % SB_UNROLL):
            nxt = following(pair)
            step(nxt, pair, diag)
            pair = nxt
        step(None, pair, diag)

    zero, one = jnp.int32(0), jnp.int32(1)
    sweep((zero, zero), nq, lambda p: (p[0] + 1, p[1] + 1), True)
    if nq > 1:
        older = lambda p: (jnp.where(p[1] > 0, p[0], p[0] + 1), jnp.where(p[1] > 0, p[1] - 1, p[0]))
        sweep((one, zero), nq * (nq - 1) // 2, older, False)

    lane = lax.broadcasted_iota(jnp.int32, (blk, LANES), 1)
    for qi in range(nq):
        o = acc_ref[0, qi]
        for i in heads[1:]:
            o = jnp.where((lane >= i * hd) & (lane < (i + 1) * hd), acc_ref[i, qi], o)
        o_ref[0, qi * blk:(qi + 1) * blk, :] = o.astype(o_ref.dtype)


def _sb_prompt(q, kt, vt, bias, *, batch, hd):
    n, d = q.shape
    seq = n // batch
    nq, blk = kt.shape[1], kt.shape[3]
    q3 = q.reshape(batch, seq, d)
    tri = (jnp.arange(blk)[:, None] > jnp.arange(blk)[None, :]).astype(BF16)
    heads = LANES // hd
    assert heads >= 2, "the logit offset rides in lanes of the tile's other heads"
    q_spec = pl.BlockSpec((1, seq, LANES), lambda b, h: (b, 0, h))
    kv_spec = pl.BlockSpec((1, nq, LANES, blk), lambda b, h: (b, 0, h, 0))
    stage = lambda dt: pltpu.VMEM((heads, blk, blk), dt)
    o = pl.pallas_call(
        functools.partial(_sb_prompt_body, blk=blk, hd=hd),
        grid=(batch, d // LANES),
        in_specs=[pl.BlockSpec(memory_space=pltpu.SMEM), q_spec, kv_spec, kv_spec, _const_spec(tri)],
        out_specs=q_spec,
        out_shape=jax.ShapeDtypeStruct((batch, seq, d), BF16),
        scratch_shapes=[stage(F32), stage(BF16),
                        pltpu.VMEM((blk, blk), F32),
                        pltpu.VMEM((heads, seq, LANES), BF16),
                        pltpu.VMEM((heads, nq, LANES, blk), BF16),
                        pltpu.VMEM((heads, nq, blk, LANES), F32),
                        pltpu.VMEM((heads, nq, blk, 1), F32)],
        compiler_params=_params("parallel", "parallel"),
    )(bias, q3, kt, vt, tri)
    return o.reshape(n, d)


def _sb_sample_body(pt_ref, qbd_ref, bias_ref, kn_ref, vn_ref, tri_ref, *rest, pages, heads, hd):
    k_refs = rest[:pages]
    v_refs = rest[pages:2 * pages]
    o_ref, acc_ref, run_ref, kpad_ref, vpad_ref = rest[2 * pages:]
    p = pl.program_id(1)
    t_new = kn_ref.shape[1]
    rows = qbd_ref.shape[1]
    psz = kpad_ref.shape[0]
    qbd = qbd_ref[0]
    bias = bias_ref[...]
    tri = tri_ref[...]

    def sweep(kps, vps, causal):
        z = [(_dot(qbd, kp) if causal is None else _dot_nt(qbd, kp)) + bias for kp in kps]
        sp = [_softplus(zi) for zi in z]
        lb = [zi - si for zi, si in zip(z, sp)]
        if causal is not None:
            sp = [jnp.where(causal, si, 0.0) for si in sp]
            lb = [jnp.where(causal, li, -jnp.inf) for li in lb]
        both = [_dot(jnp.concatenate(_split_bf16(si), axis=1), tri) for si in sp]
        run = run_ref[...]
        acc = acc_ref[...]
        for li, bi, vp in zip(lb, both, vps):
            a = jnp.exp(li - bi[:, :psz] + run).astype(BF16)
            run = run - bi[:, psz:]
            acc = acc + (_dot_nt(a, vp) if causal is None else _dot(a, vp))
        run_ref[...] = run
        acc_ref[...] = acc

    @pl.when(p == 0)
    def _():
        acc_ref[...] = jnp.zeros_like(acc_ref)
        run_ref[...] = jnp.zeros_like(run_ref)
        kpad_ref[...] = jnp.zeros_like(kpad_ref)
        vpad_ref[...] = jnp.zeros_like(vpad_ref)
        kpad_ref[0:t_new, :] = kn_ref[0]
        vpad_ref[0:t_new, :] = vn_ref[0]
        tq = lax.broadcasted_iota(jnp.int32, (rows, psz), 0) % t_new
        causal = lax.broadcasted_iota(jnp.int32, (rows, psz), 1) < tq
        sweep([kpad_ref[...].astype(BF16)], [vpad_ref[...].astype(BF16)], causal)

    sweep([r[0].astype(BF16) for r in k_refs], [r[0].astype(BF16) for r in v_refs], None)

    @pl.when(p == pl.num_programs(1) - 1)
    def _():
        lane = lax.broadcasted_iota(jnp.int32, (t_new, LANES), 1)
        per_tile = LANES // hd
        for gidx in range(heads // per_tile):
            cols = slice(gidx * LANES, (gidx + 1) * LANES)
            o = acc_ref[gidx * per_tile * t_new:(gidx * per_tile + 1) * t_new, cols]
            for i in range(1, per_tile):
                h = gidx * per_tile + i
                o = jnp.where(lane >= i * hd, acc_ref[h * t_new:(h + 1) * t_new, cols], o)
            o_ref[0, :, cols] = o


def _sb_sample(q, k_new, v_new, bias, cache_k, cache_v, layer, page_table, *, heads, hd):
    db, t_new, d = q.shape
    n_layers, n_phys, psz = cache_k.shape[:3]
    n_pages = page_table.shape[1]
    pages = PAGES_PER_STEP if n_pages % PAGES_PER_STEP == 0 else 1
    ck = cache_k.transpose(0, 1, 3, 4, 2).reshape(n_layers * n_phys, d, psz)
    cv = cache_v.transpose(0, 1, 3, 4, 2).reshape(n_layers * n_phys, d, psz)
    head_of_lane = jnp.arange(d) // hd
    qbd = jnp.where(head_of_lane[None, None, None, :] == jnp.arange(heads)[None, :, None, None],
                    q[:, None, :, :], jnp.zeros((), q.dtype)).reshape(db, heads * t_new, d)
    bias_rows = jnp.broadcast_to(jnp.repeat(bias.astype(F32), t_new)[:, None], (heads * t_new, psz))
    tri = jnp.concatenate([(jnp.arange(psz)[:, None] > jnp.arange(psz)[None, :]).astype(BF16),
                           jnp.ones((psz, psz), BF16)], axis=1)
    tri = jnp.tile(tri, (2, 1))
    base = layer * n_phys

    def page_spec(i):
        return pl.BlockSpec(
            (1, d, psz), lambda b, p, pt: (pt[b, n_pages - 1 - (p * pages + i)] + base, 0, 0))

    per_seq = lambda shape: pl.BlockSpec(shape, lambda b, p, pt: (b, 0, 0))
    const = lambda a: pl.BlockSpec(a.shape, lambda b, p, pt: (0,) * a.ndim)
    grid_spec = pltpu.PrefetchScalarGridSpec(
        num_scalar_prefetch=1,
        grid=(db, n_pages // pages),
        in_specs=[per_seq((1, heads * t_new, d)), const(bias_rows),
                  per_seq((1, t_new, d)), per_seq((1, t_new, d)), const(tri)]
                 + [page_spec(i) for i in range(pages)] * 2,
        out_specs=per_seq((1, t_new, d)),
        scratch_shapes=[pltpu.VMEM((heads * t_new, d), F32), pltpu.VMEM((heads * t_new, psz), F32),
                        pltpu.VMEM((psz, d), F32), pltpu.VMEM((psz, d), F32)],
    )
    return pl.pallas_call(
        functools.partial(_sb_sample_body, pages=pages, heads=heads, hd=hd),
        grid_spec=grid_spec,
        out_shape=jax.ShapeDtypeStruct((db, t_new, d), F32),
        compiler_params=_params("parallel", "arbitrary"),
    )(page_table, qbd, bias_rows, k_new, v_new, tri, *([ck] * pages), *([cv] * pages))


def kernel(x_prompt, x_sample, state_gla, cache_k, cache_v, page_table, p_prompt, p_sample,
           norm_ffn1, ffn1_w_in, ffn1_w_out, norm_mix, gla_w_in, gla_w_a2, gla_b_a, gla_norm,
           gla_w_out, sb_w_qkv, sb_bias, sb_w_out, norm_ffn2, ffn2_w_in, ffn2_w_out, norm_pe,
           pe_w_gate, pe_w_proj, norm_final):
    bp, tp, d = x_prompt.shape
    bs, ts, _ = x_sample.shape
    depth = norm_ffn1.shape[0]
    n_mixers = 2
    gla_heads, gla_dk, gla_dv = state_gla.shape[2:]
    dk_total, dv_total = gla_heads * gla_dk, gla_heads * gla_dv
    rank = gla_w_a2.shape[1]
    sb_heads, sb_hd = cache_k.shape[3:]
    row = lambda a: a.reshape(1, -1)

    groups = [dict(x=x_prompt.reshape(bp * tp, d), p=p_prompt.reshape(depth * bp * tp, -1), batch=bp),
              dict(x=x_sample.reshape(bs * ts, d), p=p_sample.reshape(depth * bs * ts, -1), batch=bs)]
    gla_states = [[], []]
    new_k = [[], []]
    new_v = [[], []]

    for i in range(depth):
        j = i // n_mixers
        w1_in, w1_out = ffn1_w_in[i].astype(BF16), ffn1_w_out[i].astype(BF16)
        w2_in, w2_out = ffn2_w_in[i].astype(BF16), ffn2_w_out[i].astype(BF16)
        wg, wp = pe_w_gate[i].astype(BF16), pe_w_proj[i].astype(BF16)
        if i % n_mixers == 0:
            w_main = gla_w_in[j, :, :2 * dk_total + 2 * dv_total].astype(BF16)
            w_a = jnp.pad(gla_w_in[j, :, 2 * dk_total + 2 * dv_total:],
                          ((0, 0), (0, LANES - rank))).astype(BF16)
            w_a2 = jnp.pad(gla_w_a2[j], ((0, LANES - rank), (0, 0))).astype(BF16)
            w_o = gla_w_out[j].astype(BF16)
        else:
            w_qkv = sb_w_qkv[j].astype(BF16)
            w_o = sb_w_out[j].astype(BF16)

        for gi, grp in enumerate(groups):
            x = grp["x"]
            is_prompt = gi == 0
            (x,) = _row_call(_ffn_body, [x], [row(norm_ffn1[i]), w1_in, w1_out], [(d, F32)], 512)
            if i % n_mixers == 0:
                q, k, v, r, g = _row_call(
                    _gla_proj_body, [x], [row(norm_mix[i]), w_main, w_a, w_a2, row(gla_b_a[j])],
                    [(dk_total, F32), (dk_total, F32), (dv_total, BF16 if is_prompt else F32),
                     (dv_total, F32), (dk_total, F32)],
                    512, dk_total=dk_total, dv_total=dv_total, q_scale=gla_dk ** -0.5)
                if is_prompt:
                    s0 = jnp.zeros((bp,) + state_gla.shape[2:], F32)
                    x, s_fin = _gla_mix(x, q, k, v, r, g, s0, row(gla_norm[j]), w_o,
                                        batch=bp, chunk=GLA_CHUNK, tile=min(512, tp))
                else:
                    x, s_fin = _gla_mix(x, q, k, v, r, g, state_gla[j].astype(F32), row(gla_norm[j]),
                                        w_o, batch=bs, chunk=ts, tile=ts)
                gla_states[gi].append(s_fin.astype(state_gla.dtype))
            else:
                if is_prompt:
                    q, kt, vt, ktb, vtb = _sb_proj_t(
                        x, row(norm_mix[i]), w_qkv[:, :d], w_qkv[:, d:2 * d].T, w_qkv[:, 2 * d:].T,
                        batch=bp, tm=512, blk=min(SB_BLOCK, tp), q_scale=sb_hd ** -0.5)
                    o = _sb_prompt(q, ktb, vtb, sb_bias[j].astype(F32), batch=bp, hd=sb_hd)
                    k, v = (a.reshape(bp, sb_heads, sb_hd, tp).transpose(0, 3, 1, 2) for a in (kt, vt))
                else:
                    q, k, v = _row_call(
                        _sb_proj_body, [x], [row(norm_mix[i]), w_qkv],
                        [(d, BF16), (d, F32), (d, F32)], 512, d=d, q_scale=sb_hd ** -0.5)
                    o = _sb_sample(q.reshape(bs, ts, d), k.reshape(bs, ts, d), v.reshape(bs, ts, d),
                                   sb_bias[j], cache_k, cache_v, j, page_table,
                                   heads=sb_heads, hd=sb_hd).reshape(bs * ts, d)
                    k, v = (a.reshape(bs, ts, sb_heads, sb_hd) for a in (k, v))
                new_k[gi].append(k.astype(cache_k.dtype))
                new_v[gi].append(v.astype(cache_v.dtype))
            has_mix = i % n_mixers != 0
            (x,) = _row_call(
                _post_mixer_body, [x] + ([o] if has_mix else []) + [(grp["p"], i * x.shape[0])],
                ([w_o] if has_mix else []) + [row(norm_ffn2[i]), w2_in, w2_out, row(norm_pe[i]), wg, wp,
                                              row(norm_final)],
                [(d, F32)], 512, has_mix=has_mix, final=(i == depth - 1))
            grp["x"] = x

    return (groups[0]["x"].reshape(bp, tp, d), groups[1]["x"].reshape(bs, ts, d),
            jnp.stack(gla_states[0]), jnp.stack(gla_states[1]),
            jnp.stack(new_k[0]), jnp.stack(new_v[0]), jnp.stack(new_k[1]), jnp.stack(new_v[1]))
```

```python
import functools

import jax
import jax.numpy as jnp
from jax import lax
from jax.experimental import pallas as pl
from jax.experimental.pallas import tpu as pltpu

F32 = jnp.float32
BF16 = jnp.bfloat16

NORM_EPS = 1e-6
LOG2E = 1.4426950408889634
GLA_TAU = 16.0
GLA_CHUNK = 128
SB_BLOCK = 256
SB_UNROLL = 17
PAGES_PER_STEP = 16
GLA_MAX_CHUNK_DECAY = 80.0
LANES = 128
SUBLANES = 8
VMEM_LIMIT = 56 * 1024 * 1024


def _params(*sem):
    return pltpu.CompilerParams(dimension_semantics=sem, vmem_limit_bytes=VMEM_LIMIT)


def _rms(x, g):
    return x * lax.rsqrt(jnp.mean(x * x, axis=-1, keepdims=True) + NORM_EPS) * g


def _dot(a, b):
    return jnp.dot(a, b, preferred_element_type=F32)


def _dot_nt(a, b):
    return lax.dot_general(a, b, (((1,), (1,)), ((), ())), preferred_element_type=F32)


def _dot_tn(a, b):
    return lax.dot_general(a, b, (((0,), (0,)), ((), ())), preferred_element_type=F32)


def _softplus(z):
    neg_abs = lax.bitcast_convert_type(
        lax.bitcast_convert_type(z, jnp.uint32) | jnp.uint32(0x80000000), F32)
    return jnp.maximum(z, 0.0) + jnp.log(1.0 + jnp.exp(neg_abs))


def _split_bf16(x):
    hi = x.astype(BF16)
    lo = (x - hi.astype(F32)).astype(BF16)
    return hi, lo


def _const_spec(a):
    nd = a.ndim
    return pl.BlockSpec(a.shape, lambda *_: (0,) * nd, pipeline_mode=pl.Buffered(1))


def _row_call(body, row_args, const_args, out_dims, tm, **kw):
    n = row_args[0].shape[0]
    tm = min(tm, n)
    assert n % tm == 0
    row_args = [a if isinstance(a, tuple) else (a, 0) for a in row_args]
    assert all(first % tm == 0 for _, first in row_args)
    in_specs = [pl.BlockSpec((tm, a.shape[1]), lambda i, off=first // tm: (i + off, 0))
                for a, first in row_args]
    row_args = [a for a, _ in row_args]
    in_specs += [_const_spec(a) for a in const_args]
    out_specs = [pl.BlockSpec((tm, d), lambda i: (i, 0)) for d, _ in out_dims]
    out_shape = [jax.ShapeDtypeStruct((n, d), dt) for d, dt in out_dims]
    outs = pl.pallas_call(
        functools.partial(body, **kw),
        grid=(n // tm,),
        in_specs=in_specs,
        out_specs=out_specs,
        out_shape=out_shape,
        compiler_params=_params("parallel"),
    )(*row_args, *const_args)
    return outs


def _ffn(x, g_ref, win_ref, wout_ref):
    xn = _rms(x, g_ref[...]).astype(BF16)
    h = _dot(xn, win_ref[...])
    dff = wout_ref.shape[0]
    gate, up = h[:, :dff], h[:, dff:]
    act = (gate * jax.nn.sigmoid(gate) * up).astype(BF16)
    return x + 0.5 * _dot(act, wout_ref[...])


def _post_mixer_body(*refs, has_mix, final):
    x_ref, *refs = refs
    x = x_ref[...]
    if has_mix:
        mix_ref, *refs = refs
    p_ref, *refs = refs
    if has_mix:
        wmix_ref, *refs = refs
        x = x + _dot(mix_ref[...].astype(BF16), wmix_ref[...])
    g_ref, win_ref, wout_ref, gpe_ref, wg_ref, wp_ref, gf_ref, o_ref = refs
    x = _ffn(x, g_ref, win_ref, wout_ref)
    xn = _rms(x, gpe_ref[...]).astype(BF16)
    gate = jax.nn.sigmoid(_dot(xn, wg_ref[...]))
    x = x + gate * _dot(p_ref[...].astype(BF16), wp_ref[...])
    if final:
        x = _rms(x, gf_ref[...])
    o_ref[...] = x


def _pre_mixer(x_ref, g1_ref, win_ref, wout_ref, x1_ref, g_ref):
    x1 = _ffn(x_ref[...], g1_ref, win_ref, wout_ref)
    x1_ref[...] = x1
    return _rms(x1, g_ref[...]).astype(BF16)


def _gla_proj_body(x_ref, g1_ref, win_ref, wout_ref, g_ref, w_ref, wa_ref, wa2_ref, ba_ref,
                   x1_ref, q_ref, k_ref, v_ref, r_ref, gate_ref, *, dk_total, dv_total, q_scale):
    xn = _pre_mixer(x_ref, g1_ref, win_ref, wout_ref, x1_ref, g_ref)
    y = _dot(xn, w_ref[...])
    q_ref[...] = y[:, :dk_total] * q_scale
    k_ref[...] = y[:, dk_total:2 * dk_total]
    v_ref[...] = y[:, 2 * dk_total:2 * dk_total + dv_total].astype(v_ref.dtype)
    r_ref[...] = y[:, 2 * dk_total + dv_total:]
    a = _dot(xn, wa_ref[...]).astype(BF16)
    ga = _dot(a, wa2_ref[...]) + ba_ref[...]
    gate_ref[...] = (jnp.minimum(ga, 0.0) - jnp.log1p(jnp.exp(-jnp.abs(ga)))) * (1.0 / GLA_TAU)


def _sb_proj_body(x_ref, g1_ref, win_ref, wout_ref, g_ref, w_ref, x1_ref, q_ref, k_ref, v_ref,
                  *, d, q_scale):
    xn = _pre_mixer(x_ref, g1_ref, win_ref, wout_ref, x1_ref, g_ref)
    y = _dot(xn, w_ref[...])
    q_ref[...] = (y[:, :d] * q_scale).astype(BF16)
    k_ref[...] = y[:, d:2 * d]
    v_ref[...] = y[:, 2 * d:]


def _sb_proj_t_body(x_ref, g1_ref, win_ref, wout_ref, g_ref, wq_ref, wkt_ref, wvt_ref,
                    x1_ref, q_ref, kt_ref, vt_ref, ktb_ref, vtb_ref, *, q_scale):
    xn = _pre_mixer(x_ref, g1_ref, win_ref, wout_ref, x1_ref, g_ref)
    q_ref[...] = (_dot(xn, wq_ref[...]) * q_scale).astype(BF16)
    blk = ktb_ref.shape[3]
    for w_ref, t_ref, tb_ref in ((wkt_ref, kt_ref, ktb_ref), (wvt_ref, vt_ref, vtb_ref)):
        yt = _dot_nt(w_ref[...], xn)
        t_ref[0] = yt
        for j in range(tb_ref.shape[1]):
            tb_ref[0, j] = yt[:, j * blk:(j + 1) * blk].astype(BF16)


def _sb_proj_t(x, ffn, g, wq, wkt, wvt, *, batch, tm, blk, q_scale):
    n, d = x.shape
    seq = n // batch
    tm = min(tm, seq)
    nt = seq // tm
    row = pl.BlockSpec((tm, d), lambda b, t: (b * nt + t, 0))
    tr = pl.BlockSpec((1, d, tm), lambda b, t: (b, 0, t))
    trb = pl.BlockSpec((1, tm // blk, d, blk), lambda b, t: (b, t, 0, 0))
    return pl.pallas_call(
        functools.partial(_sb_proj_t_body, q_scale=q_scale),
        grid=(batch, nt),
        in_specs=[row] + [_const_spec(a) for a in (*ffn, g, wq, wkt, wvt)],
        out_specs=[row, row, tr, tr, trb, trb],
        out_shape=[jax.ShapeDtypeStruct((n, d), F32),
                   jax.ShapeDtypeStruct((n, d), BF16),
                   jax.ShapeDtypeStruct((batch, d, seq), F32),
                   jax.ShapeDtypeStruct((batch, d, seq), F32),
                   jax.ShapeDtypeStruct((batch, seq // blk, d, blk), BF16),
                   jax.ShapeDtypeStruct((batch, seq // blk, d, blk), BF16)],
        compiler_params=_params("parallel", "parallel"),
    )(x, *ffn, g, wq, wkt, wvt)


def _gla_mix_body(x_ref, q_ref, k_ref, v_ref, r_ref, g_ref, s0_ref, gn_ref, wo_ref, tri_ref,
                  y_ref, sfin_ref, st_ref, o_ref, b_ref, qt_ref, kt_ref, ks_ref, dec_ref,
                  *, heads, chunk):
    t = pl.program_id(1)
    tile = x_ref.shape[0]
    dk = q_ref.shape[1] // heads
    dv = v_ref.shape[1] // heads

    @pl.when(t == 0)
    def _():
        for h in range(heads):
            st_ref[h] = s0_ref[0, h].T

    tri = tri_ref[...]
    causal = (lax.broadcasted_iota(jnp.int32, (chunk, chunk), 1)
              <= lax.broadcasted_iota(jnp.int32, (chunk, chunk), 0))

    n_chunks = tile // chunk
    kd = heads * dk

    for c in range(n_chunks):
        rows = slice(c * chunk, (c + 1) * chunk)
        g_hi, g_lo = _split_bf16(g_ref[rows, :])
        b_ref[rows, :] = _dot(tri, g_hi) + _dot(tri, g_lo)
    b3 = b_ref[...].reshape(n_chunks, chunk, kd)
    b_last = b3[:, chunk - 1:chunk, :]
    q3 = q_ref[...].reshape(n_chunks, chunk, kd)
    k3 = k_ref[...].reshape(n_chunks, chunk, kd)
    qt_ref[...] = (q3 * jnp.exp(b3)).astype(BF16).reshape(tile, kd)
    ks_ref[...] = (k3 * jnp.exp(b_last - b3)).astype(BF16).reshape(tile, kd)
    dec_ref[...] = jnp.broadcast_to(jnp.exp(b_last), (n_chunks, SUBLANES, kd))
    factorable = jnp.min(b_last) >= -GLA_MAX_CHUNK_DECAY

    def chunk_step(c, carry, *, factorable):
        rows = pl.ds(pl.multiple_of(c * chunk, chunk), chunk)
        q_t = qt_ref[rows, :]
        k_s = ks_ref[rows, :]
        v = v_ref[rows, :].astype(BF16)
        s_t = [st_ref[h] for h in range(heads)]
        hk = [slice(h * dk, (h + 1) * dk) for h in range(heads)]
        hv = [slice(h * dv, (h + 1) * dv) for h in range(heads)]
        o = [_dot_nt(q_t[:, hk[h]], s_t[h].astype(BF16)) for h in range(heads)]
        if factorable:
            k_t = kt_ref[rows, :]
            sc = [jnp.where(causal, _dot_nt(q_t[:, hk[h]], k_t[:, hk[h]]), 0.0).astype(BF16)
                  for h in range(heads)]
        upd = [_dot_tn(v[:, hv[h]], k_s[:, hk[h]]) for h in range(heads)]
        if factorable:
            o = [o[h] + _dot(sc[h], v[:, hv[h]]) for h in range(heads)]
        dec = dec_ref[c]
        for h in range(heads):
            o_ref[rows, hv[h]] = o[h]
            st_ref[h] = s_t[h] * dec[0:1, hk[h]] + upd[h]
        if factorable:
            return carry

        b = b_ref[rows, :]
        k = k_ref[rows, :]
        s_idx = lax.broadcasted_iota(jnp.int32, (chunk, 1), 0)
        sub = lax.broadcasted_iota(jnp.int32, (SUBLANES, 1), 0)

        def token_group(t8, carry):
            rows8 = pl.ds(pl.multiple_of(c * chunk + t8 * SUBLANES, SUBLANES), SUBLANES)
            q8 = q_ref[rows8, :]
            b8 = b_ref[rows8, :]
            out = [jnp.zeros((SUBLANES, dv), F32) for _ in range(heads)]
            for r in range(SUBLANES):
                t = t8 * SUBLANES + r
                decay = jnp.exp(jnp.where(s_idx <= t, b8[r:r + 1, :] - b, -jnp.inf))
                w = q8[r:r + 1, :] * decay * k
                for h in range(heads):
                    sc = jnp.sum(w[:, h * dk:(h + 1) * dk], axis=1, keepdims=True)
                    v_h = v_ref[rows, h * dv:(h + 1) * dv].astype(F32)
                    o_t = jnp.sum(sc * v_h, axis=0, keepdims=True)
                    out[h] = jnp.where(sub == r, o_t, out[h])
            for h in range(heads):
                o_ref[rows8, h * dv:(h + 1) * dv] += out[h]
            return carry

        return lax.fori_loop(0, chunk // SUBLANES, token_group, carry)

    for flag, pred in ((True, factorable), (False, jnp.logical_not(factorable))):
        @pl.when(pred)
        def _():
            if flag:
                kt_ref[...] = (k3 * jnp.exp(-b3)).astype(BF16).reshape(tile, kd)
            lax.fori_loop(0, n_chunks, functools.partial(chunk_step, factorable=flag), 0)

    gn = gn_ref[...]
    r = r_ref[...]
    parts = []
    for h in range(heads):
        vs = slice(h * dv, (h + 1) * dv)
        parts.append(_rms(o_ref[:, vs], gn[:, vs]))
    on = jnp.concatenate(parts, axis=1)
    on = (on * (r * jax.nn.sigmoid(r))).astype(BF16)
    y_ref[...] = x_ref[...] + _dot(on, wo_ref[...])

    @pl.when(t == pl.num_programs(1) - 1)
    def _():
        for h in range(heads):
            sfin_ref[0, h] = st_ref[h].T


def _gla_mix(x, q, k, v, r, g, s0, gn, wo, *, batch, chunk, tile):
    n, d = x.shape
    seq = n // batch
    heads, dk, dv = s0.shape[1:]
    nt = seq // tile
    tri = jnp.tril(jnp.ones((chunk, chunk), F32)).astype(BF16)
    row = lambda w: pl.BlockSpec((tile, w), lambda b, t: (b * nt + t, 0))
    y, sfin = pl.pallas_call(
        functools.partial(_gla_mix_body, heads=heads, chunk=chunk),
        grid=(batch, nt),
        in_specs=[row(d), row(heads * dk), row(heads * dk), row(heads * dv), row(heads * dv),
                  row(heads * dk),
                  pl.BlockSpec((1, heads, dk, dv), lambda b, t: (b, 0, 0, 0)),
                  _const_spec(gn), _const_spec(wo), _const_spec(tri)],
        out_specs=[row(d), pl.BlockSpec((1, heads, dk, dv), lambda b, t: (b, 0, 0, 0))],
        out_shape=[jax.ShapeDtypeStruct((n, d), F32),
                   jax.ShapeDtypeStruct((batch, heads, dk, dv), F32)],
        scratch_shapes=[pltpu.VMEM((heads, dv, dk), F32), pltpu.VMEM((tile, heads * dv), F32),
                        pltpu.VMEM((tile, heads * dk), F32)]
                       + [pltpu.VMEM((tile, heads * dk), BF16)] * 3
                       + [pltpu.VMEM((tile // chunk, SUBLANES, heads * dk), F32)],
        compiler_params=_params("arbitrary", "arbitrary"),
    )(x, q, k, v, r, g, s0, gn, wo, tri)
    return y, sfin


def _sb_prompt_body(bias_ref, q_ref, k_ref, v_ref, tri_ref, o_ref,
                    lb_ref, hi_ref, zm_ref, qh_ref, kh_ref, acc_ref, run_ref, *, blk, hd):
    h2 = pl.program_id(1)
    nq = k_ref.shape[1]
    tri = tri_ref[...]
    row = lax.broadcasted_iota(jnp.int32, (blk, blk), 0)
    col = lax.broadcasted_iota(jnp.int32, (blk, blk), 1)
    heads = range(LANES // hd)

    zm_ref[...] = jnp.where(col < row, 0.0, -jnp.inf)
    q_all = q_ref[0].astype(F32)
    k_all = k_ref[0].astype(F32)
    q_lane = lax.broadcasted_iota(jnp.int32, q_all.shape, 1)
    k_row = lax.broadcasted_iota(jnp.int32, k_all.shape, 1)
    for i in heads:
        b = jnp.full(q_all.shape, bias_ref[h2 * len(heads) + i], F32)
        b_hi = b.astype(BF16).astype(F32)
        spare = lambda lane: (lane + (LANES - i * hd)) % LANES - hd
        own = lambda lane: (lane >= i * hd) & (lane < (i + 1) * hd)
        q_spare = jnp.where(spare(q_lane) == 0, b_hi, jnp.where(spare(q_lane) == 1, b - b_hi, 0.0))
        qh_ref[i] = jnp.where(own(q_lane), q_all, q_spare).astype(BF16)
        k_spare = jnp.where(spare(k_row) < 2, 1.0, 0.0)
        kh_ref[i] = jnp.where(own(k_row), k_all, k_spare).astype(BF16)

    def step(front, back, diag):
        if back is not None:
            qb, kb = back
            vt = v_ref[0, kb]
            hi = [hi_ref[i] for i in heads]
            later = [_dot(hi[i], tri) for i in heads]
        if front is not None:
            qf, kf = front
            q_rows = pl.ds(pl.multiple_of(qf * blk, blk), blk)

        def logits(i):
            z = _dot(qh_ref[i, q_rows, :], kh_ref[i, kf])
            return z + zm_ref[...] if diag else z

        def finish(i):
            total = later[i][:, 0:1] + hi[i][:, 0:1].astype(F32)
            if diag:
                a = jnp.exp(lb_ref[i] - later[i]).astype(BF16)
                run_ref[i, qb] = -total
                acc_ref[i, qb] = _dot_nt(a, vt)
            else:
                run = run_ref[i, qb]
                a = jnp.exp(lb_ref[i] - later[i] + run).astype(BF16)
                run_ref[i, qb] = run - total
                acc_ref[i, qb] += _dot_nt(a, vt)

        def soft(i, z):
            s = _softplus(z)
            lb_ref[i] = z - s
            hi_ref[i] = s.astype(BF16)

        z = logits(0) if front is not None else None
        for i in heads:
            if back is not None:
                finish(i)
            if front is not None:
                if i > 0:
                    z = logits(i)
                soft(i, z)

    def sweep(first, n, following, diag):
        step(first, None, diag)

        def unrolled(_, pair):
            for _ in range(SB_UNROLL):
                nxt = following(pair)
                step(nxt, pair, diag)
                pair = nxt
            return pair

        pair = lax.fori_loop(0, (n - 1) // SB_UNROLL, unrolled, first)
        for _ in range((n - 1) % SB_UNROLL):
            nxt = following(pair)
            step(nxt, pair, diag)
            pair = nxt
        step(None, pair, diag)

    zero, one = jnp.int32(0), jnp.int32(1)
    sweep((zero, zero), nq, lambda p: (p[0] + 1, p[1] + 1), True)
    if nq > 1:
        older = lambda p: (jnp.where(p[1] > 0, p[0], p[0] + 1), jnp.where(p[1] > 0, p[1] - 1, p[0]))
        sweep((one, zero), nq * (nq - 1) // 2, older, False)

    lane = lax.broadcasted_iota(jnp.int32, (blk, LANES), 1)
    for qi in range(nq):
        o = acc_ref[0, qi]
        for i in heads[1:]:
            o = jnp.where((lane >= i * hd) & (lane < (i + 1) * hd), acc_ref[i, qi], o)
        o_ref[0, qi * blk:(qi + 1) * blk, :] = o.astype(o_ref.dtype)


def _sb_prompt(q, kt, vt, bias, *, batch, hd):
    n, d = q.shape
    seq = n // batch
    nq, blk = kt.shape[1], kt.shape[3]
    q3 = q.reshape(batch, seq, d)
    tri = (jnp.arange(blk)[:, None] > jnp.arange(blk)[None, :]).astype(BF16)
    heads = LANES // hd
    assert heads >= 2, "the logit offset rides in lanes of the tile's other heads"
    q_spec = pl.BlockSpec((1, seq, LANES), lambda b, h: (b, 0, h))
    kv_spec = pl.BlockSpec((1, nq, LANES, blk), lambda b, h: (b, 0, h, 0))
    stage = lambda dt: pltpu.VMEM((heads, blk, blk), dt)
    o = pl.pallas_call(
        functools.partial(_sb_prompt_body, blk=blk, hd=hd),
        grid=(batch, d // LANES),
        in_specs=[pl.BlockSpec(memory_space=pltpu.SMEM), q_spec, kv_spec, kv_spec, _const_spec(tri)],
        out_specs=q_spec,
        out_shape=jax.ShapeDtypeStruct((batch, seq, d), BF16),
        scratch_shapes=[stage(F32), stage(BF16),
                        pltpu.VMEM((blk, blk), F32),
                        pltpu.VMEM((heads, seq, LANES), BF16),
                        pltpu.VMEM((heads, nq, LANES, blk), BF16),
                        pltpu.VMEM((heads, nq, blk, LANES), F32),
                        pltpu.VMEM((heads, nq, blk, 1), F32)],
        compiler_params=_params("parallel", "parallel"),
    )(bias, q3, kt, vt, tri)
    return o.reshape(n, d)


def _sb_sample_body(pt_ref, qbd_ref, bias_ref, kn_ref, vn_ref, tri_ref, *rest, pages, heads, hd):
    k_refs = rest[:pages]
    v_refs = rest[pages:2 * pages]
    o_ref, acc_ref, run_ref, kpad_ref, vpad_ref = rest[2 * pages:]
    p = pl.program_id(1)
    t_new = kn_ref.shape[1]
    rows = qbd_ref.shape[1]
    psz = kpad_ref.shape[0]
    qbd = qbd_ref[0]
    bias = bias_ref[...]
    tri = tri_ref[...]

    def sweep(kps, vps, causal):
        z = [(_dot(qbd, kp) if causal is None else _dot_nt(qbd, kp)) + bias for kp in kps]
        sp = [_softplus(zi) for zi in z]
        lb = [zi - si for zi, si in zip(z, sp)]
        if causal is not None:
            sp = [jnp.where(causal, si, 0.0) for si in sp]
            lb = [jnp.where(causal, li, -jnp.inf) for li in lb]
        both = [_dot(jnp.concatenate(_split_bf16(si), axis=1), tri) for si in sp]
        run = run_ref[...]
        acc = acc_ref[...]
        for li, bi, vp in zip(lb, both, vps):
            a = jnp.exp(li - bi[:, :psz] + run).astype(BF16)
            run = run - bi[:, psz:]
            acc = acc + (_dot_nt(a, vp) if causal is None else _dot(a, vp))
        run_ref[...] = run
        acc_ref[...] = acc

    @pl.when(p == 0)
    def _():
        acc_ref[...] = jnp.zeros_like(acc_ref)
        run_ref[...] = jnp.zeros_like(run_ref)
        kpad_ref[...] = jnp.zeros_like(kpad_ref)
        vpad_ref[...] = jnp.zeros_like(vpad_ref)
        kpad_ref[0:t_new, :] = kn_ref[0]
        vpad_ref[0:t_new, :] = vn_ref[0]
        tq = lax.broadcasted_iota(jnp.int32, (rows, psz), 0) % t_new
        causal = lax.broadcasted_iota(jnp.int32, (rows, psz), 1) < tq
        sweep([kpad_ref[...].astype(BF16)], [vpad_ref[...].astype(BF16)], causal)

    sweep([r[0].astype(BF16) for r in k_refs], [r[0].astype(BF16) for r in v_refs], None)

    @pl.when(p == pl.num_programs(1) - 1)
    def _():
        lane = lax.broadcasted_iota(jnp.int32, (t_new, LANES), 1)
        per_tile = LANES // hd
        for gidx in range(heads // per_tile):
            cols = slice(gidx * LANES, (gidx + 1) * LANES)
            o = acc_ref[gidx * per_tile * t_new:(gidx * per_tile + 1) * t_new, cols]
            for i in range(1, per_tile):
                h = gidx * per_tile + i
                o = jnp.where(lane >= i * hd, acc_ref[h * t_new:(h + 1) * t_new, cols], o)
            o_ref[0, :, cols] = o


def _sb_sample(q, k_new, v_new, bias, cache_k, cache_v, layer, page_table, *, heads, hd):
    db, t_new, d = q.shape
    n_layers, n_phys, psz = cache_k.shape[:3]
    n_pages = page_table.shape[1]
    pages = PAGES_PER_STEP if n_pages % PAGES_PER_STEP == 0 else 1
    ck = cache_k.transpose(0, 1, 3, 4, 2).reshape(n_layers * n_phys, d, psz)
    cv = cache_v.transpose(0, 1, 3, 4, 2).reshape(n_layers * n_phys, d, psz)
    head_of_lane = jnp.arange(d) // hd
    qbd = jnp.where(head_of_lane[None, None, None, :] == jnp.arange(heads)[None, :, None, None],
                    q[:, None, :, :], jnp.zeros((), q.dtype)).reshape(db, heads * t_new, d)
    bias_rows = jnp.broadcast_to(jnp.repeat(bias.astype(F32), t_new)[:, None], (heads * t_new, psz))
    tri = jnp.concatenate([(jnp.arange(psz)[:, None] > jnp.arange(psz)[None, :]).astype(BF16),
                           jnp.ones((psz, psz), BF16)], axis=1)
    tri = jnp.tile(tri, (2, 1))
    base = layer * n_phys

    def page_spec(i):
        return pl.BlockSpec(
            (1, d, psz), lambda b, p, pt: (pt[b, n_pages - 1 - (p * pages + i)] + base, 0, 0))

    per_seq = lambda shape: pl.BlockSpec(shape, lambda b, p, pt: (b, 0, 0))
    const = lambda a: pl.BlockSpec(a.shape, lambda b, p, pt: (0,) * a.ndim)
    grid_spec = pltpu.PrefetchScalarGridSpec(
        num_scalar_prefetch=1,
        grid=(db, n_pages // pages),
        in_specs=[per_seq((1, heads * t_new, d)), const(bias_rows),
                  per_seq((1, t_new, d)), per_seq((1, t_new, d)), const(tri)]
                 + [page_spec(i) for i in range(pages)] * 2,
        out_specs=per_seq((1, t_new, d)),
        scratch_shapes=[pltpu.VMEM((heads * t_new, d), F32), pltpu.VMEM((heads * t_new, psz), F32),
                        pltpu.VMEM((psz, d), F32), pltpu.VMEM((psz, d), F32)],
    )
    return pl.pallas_call(
        functools.partial(_sb_sample_body, pages=pages, heads=heads, hd=hd),
        grid_spec=grid_spec,
        out_shape=jax.ShapeDtypeStruct((db, t_new, d), F32),
        compiler_params=_params("parallel", "arbitrary"),
    )(page_table, qbd, bias_rows, k_new, v_new, tri, *([ck] * pages), *([cv] * pages))


def kernel(x_prompt, x_sample, state_gla, cache_k, cache_v, page_table, p_prompt, p_sample,
           norm_ffn1, ffn1_w_in, ffn1_w_out, norm_mix, gla_w_in, gla_w_a2, gla_b_a, gla_norm,
           gla_w_out, sb_w_qkv, sb_bias, sb_w_out, norm_ffn2, ffn2_w_in, ffn2_w_out, norm_pe,
           pe_w_gate, pe_w_proj, norm_final):
    bp, tp, d = x_prompt.shape
    bs, ts, _ = x_sample.shape
    depth = norm_ffn1.shape[0]
    n_mixers = 2
    gla_heads, gla_dk, gla_dv = state_gla.shape[2:]
    dk_total, dv_total = gla_heads * gla_dk, gla_heads * gla_dv
    rank = gla_w_a2.shape[1]
    sb_heads, sb_hd = cache_k.shape[3:]
    row = lambda a: a.reshape(1, -1)

    groups = [dict(x=x_prompt.reshape(bp * tp, d), p=p_prompt.reshape(depth * bp * tp, -1), batch=bp),
              dict(x=x_sample.reshape(bs * ts, d), p=p_sample.reshape(depth * bs * ts, -1), batch=bs)]
    gla_states = [[], []]
    new_k = [[], []]
    new_v = [[], []]

    for i in range(depth):
        j = i // n_mixers
        w1_in, w1_out = ffn1_w_in[i].astype(BF16), ffn1_w_out[i].astype(BF16)
        w2_in, w2_out = ffn2_w_in[i].astype(BF16), ffn2_w_out[i].astype(BF16)
        wg, wp = pe_w_gate[i].astype(BF16), pe_w_proj[i].astype(BF16)
        if i % n_mixers == 0:
            w_main = gla_w_in[j, :, :2 * dk_total + 2 * dv_total].astype(BF16)
            w_a = jnp.pad(gla_w_in[j, :, 2 * dk_total + 2 * dv_total:],
                          ((0, 0), (0, LANES - rank))).astype(BF16)
            w_a2 = jnp.pad(gla_w_a2[j], ((0, LANES - rank), (0, 0))).astype(BF16)
            w_o = gla_w_out[j].astype(BF16)
        else:
            w_qkv = sb_w_qkv[j].astype(BF16)
            w_o = sb_w_out[j].astype(BF16)

        for gi, grp in enumerate(groups):
            x = grp["x"]
            is_prompt = gi == 0
            ffn1 = [row(norm_ffn1[i]), w1_in, w1_out]
            if i % n_mixers == 0:
                x, q, k, v, r, g = _row_call(
                    _gla_proj_body, [x], ffn1 + [row(norm_mix[i]), w_main, w_a, w_a2, row(gla_b_a[j])],
                    [(d, F32), (dk_total, F32), (dk_total, F32), (dv_total, BF16 if is_prompt else F32),
                     (dv_total, F32), (dk_total, F32)],
                    512, dk_total=dk_total, dv_total=dv_total, q_scale=gla_dk ** -0.5)
                if is_prompt:
                    s0 = jnp.zeros((bp,) + state_gla.shape[2:], F32)
                    x, s_fin = _gla_mix(x, q, k, v, r, g, s0, row(gla_norm[j]), w_o,
                                        batch=bp, chunk=GLA_CHUNK, tile=min(512, tp))
                else:
                    x, s_fin = _gla_mix(x, q, k, v, r, g, state_gla[j].astype(F32), row(gla_norm[j]),
                                        w_o, batch=bs, chunk=ts, tile=ts)
                gla_states[gi].append(s_fin.astype(state_gla.dtype))
            else:
                if is_prompt:
                    x, q, kt, vt, ktb, vtb = _sb_proj_t(
                        x, ffn1, row(norm_mix[i]), w_qkv[:, :d], w_qkv[:, d:2 * d].T, w_qkv[:, 2 * d:].T,
                        batch=bp, tm=512, blk=min(SB_BLOCK, tp), q_scale=sb_hd ** -0.5)
                    o = _sb_prompt(q, ktb, vtb, sb_bias[j].astype(F32), batch=bp, hd=sb_hd)
                    k, v = (a.reshape(bp, sb_heads, sb_hd, tp).transpose(0, 3, 1, 2) for a in (kt, vt))
                else:
                    x, q, k, v = _row_call(
                        _sb_proj_body, [x], ffn1 + [row(norm_mix[i]), w_qkv],
                        [(d, F32), (d, BF16), (d, F32), (d, F32)], 512, d=d, q_scale=sb_hd ** -0.5)
                    o = _sb_sample(q.reshape(bs, ts, d), k.reshape(bs, ts, d), v.reshape(bs, ts, d),
                                   sb_bias[j], cache_k, cache_v, j, page_table,
                                   heads=sb_heads, hd=sb_hd).reshape(bs * ts, d)
                    k, v = (a.reshape(bs, ts, sb_heads, sb_hd) for a in (k, v))
                new_k[gi].append(k.astype(cache_k.dtype))
                new_v[gi].append(v.astype(cache_v.dtype))
            has_mix = i % n_mixers != 0
            (x,) = _row_call(
                _post_mixer_body, [x] + ([o] if has_mix else []) + [(grp["p"], i * x.shape[0])],
                ([w_o] if has_mix else []) + [row(norm_ffn2[i]), w2_in, w2_out, row(norm_pe[i]), wg, wp,
                                              row(norm_final)],
                [(d, F32)], 512, has_mix=has_mix, final=(i == depth - 1))
            grp["x"] = x

    return (groups[0]["x"].reshape(bp, tp, d), groups[1]["x"].reshape(bs, ts, d),
            jnp.stack(gla_states[0]), jnp.stack(gla_states[1]),
            jnp.stack(new_k[0]), jnp.stack(new_v[0]), jnp.stack(new_k[1]), jnp.stack(new_v[1]))
```
